```python
import math
import jax, jax.numpy as jnp
from jax import lax
import numpy as np

D_MODEL = 1024
BATCH = 4
SEQ = 8192
DEPTH = 2
DEC_BATCH = 32
DEC_SEQ = 8
PAST_LEN = 16384
PAGE_SIZE = 128

N_MIXERS = 2
N_ATTN_LAYERS = (DEPTH + 1) // 2
N_SSM_LAYERS = DEPTH // 2
N_HEADS = 16
HEAD_DIM = D_MODEL // N_HEADS
N_KV = 4
GROUP_R = N_HEADS // N_KV
ROT_DIM = HEAD_DIM // 4
ROPE_THETA = 500000.0
CMP_LEN = 32
CMP_STRIDE = 16
CMP_RATIO = CMP_LEN // CMP_STRIDE
CMP_HIDDEN = 64
SEL_LEN = 64
N_SEL = 16
WINDOW = 512
Q_BLOCK = 128
ATTN_WIDTH = N_HEADS * HEAD_DIM
KV_WIDTH = N_KV * HEAD_DIM
ATTN_IN = 2 * ATTN_WIDTH + 6 * KV_WIDTH + 3 * N_HEADS
S5_WIDTH = D_MODEL
S5_GROUP = 16
S5_NGROUPS = S5_WIDTH // S5_GROUP
S5_STATE = 64
NEG = -1e30
SEL_FORCE = 1e4
RMS_EPS = 1e-6

kernel_name = 'nsa_s5_hybrid_step'


def rms_norm(x, g):
    xf = x.astype(jnp.float32)
    y = xf * lax.rsqrt(jnp.mean(xf * xf, axis=-1, keepdims=True) + RMS_EPS)
    return (y * g.astype(jnp.float32)).astype(x.dtype)


def modulate(x, c, g_pre, w_ada, b_ada):
    mod = c @ w_ada + b_ada
    shift, scale, gate = jnp.split(mod, 3, axis=-1)
    h = rms_norm(x, g_pre) * (1 + scale[:, None]) + shift[:, None]
    return h, gate


def rope(x, pos):
    half = ROT_DIM // 2
    inv = ROPE_THETA ** (-jnp.arange(half, dtype=jnp.float32) / half)
    ang = pos.astype(jnp.float32)[:, None] * inv[None]
    cos = jnp.cos(ang)[None, :, None, :].astype(x.dtype)
    sin = jnp.sin(ang)[None, :, None, :].astype(x.dtype)
    x1, x2, rest = x[..., :half], x[..., half:ROT_DIM], x[..., ROT_DIM:]
    return jnp.concatenate([x1 * cos - x2 * sin, x1 * sin + x2 * cos, rest], axis=-1)


def masked_softmax(s, mask):
    s = jnp.where(mask, s.astype(jnp.float32), NEG)
    p = jax.nn.softmax(s, axis=-1)
    return jnp.where(mask, p, 0.0)


def compress(k, pe, w1, w2):
    b, L = k.shape[:2]
    n_ch = L // CMP_STRIDE
    nc = n_ch - CMP_RATIO + 1
    chunks = k[:, :n_ch * CMP_STRIDE].reshape(b, n_ch, CMP_STRIDE, N_KV, HEAD_DIM)
    w1r = w1.reshape(CMP_RATIO, CMP_STRIDE, HEAD_DIM, CMP_HIDDEN)
    part = jnp.einsum('bnsgd,rsdh->bngrh', chunks, w1r)
    pre = part[:, 0:nc, :, 0]
    for r in range(1, CMP_RATIO):
        pre = pre + part[:, r:r + nc, :, r]
    pe_bias = jnp.einsum('cd,cdh->h', pe, w1.reshape(CMP_LEN, HEAD_DIM, CMP_HIDDEN))
    return jnp.einsum('bngh,hd->bngd', jax.nn.silu(pre + pe_bias), w2)


def sel_blocks(k):
    b, L = k.shape[:2]
    ns = -(-L // SEL_LEN)
    k = jnp.pad(k, ((0, 0), (0, ns * SEL_LEN - L), (0, 0), (0, 0)))
    return k.reshape(b, ns, SEL_LEN, N_KV, HEAD_DIM).transpose(0, 3, 1, 2, 4)


def nsa_core(q, q_pos, g_logit, kc, vc, ksb, vsb, kw, vw, kw_pos):
    b, tq = q.shape[:2]
    qg = q.reshape(b, tq, N_KV, GROUP_R, HEAD_DIM)
    scale = HEAD_DIM ** -0.5
    nc = kc.shape[1]
    c_start = jnp.arange(nc) * CMP_STRIDE
    m_c = (c_start + CMP_LEN - 1)[None, :] <= q_pos[:, None]
    s_c = jnp.einsum('btgrd,bngd->bgrtn', qg, kc) * scale
    p_c = masked_softmax(s_c, m_c)
    o_c = jnp.einsum('bgrtn,bngd->btgrd', p_c.astype(vc.dtype), vc)
    ns = ksb.shape[2]
    s_start = jnp.arange(ns) * SEL_LEN
    overlap = ((c_start[:, None] < s_start[None] + SEL_LEN) & (c_start[:, None] + CMP_LEN > s_start[None])).astype(jnp.float32)
    imp = jnp.einsum('bgrtn,nj->bgtj', p_c, overlap)
    cur = q_pos // SEL_LEN
    j = jnp.arange(ns)
    forced = (j[None] == 0) | (j[None] == cur[:, None]) | (j[None] == cur[:, None] - 1)
    valid = s_start[None] <= q_pos[:, None]
    imp = jnp.where(valid, imp + jnp.where(forced, SEL_FORCE, 0.0), -SEL_FORCE)
    n_sel = min(N_SEL, ns)
    _, idx = lax.top_k(imp, n_sel)
    flat = idx.reshape(b, N_KV, tq * n_sel, 1, 1)
    k_sel = jnp.take_along_axis(ksb, flat, axis=2).reshape(b, N_KV, tq, n_sel, SEL_LEN, HEAD_DIM)
    v_sel = jnp.take_along_axis(vsb, flat, axis=2).reshape(b, N_KV, tq, n_sel, SEL_LEN, HEAD_DIM)
    pos_sel = idx[..., None] * SEL_LEN + jnp.arange(SEL_LEN)
    m_s = (pos_sel <= q_pos[:, None, None]).reshape(b, N_KV, 1, tq, n_sel * SEL_LEN)
    s_s = jnp.einsum('btgrd,bgtnkd->bgrtnk', qg, k_sel).reshape(b, N_KV, GROUP_R, tq, n_sel * SEL_LEN) * scale
    p_s = masked_softmax(s_s, m_s).reshape(b, N_KV, GROUP_R, tq, n_sel, SEL_LEN)
    o_s = jnp.einsum('bgrtnk,bgtnkd->btgrd', p_s.astype(v_sel.dtype), v_sel)
    m_w = (kw_pos[None] <= q_pos[:, None]) & (kw_pos[None] >= q_pos[:, None] - WINDOW) & (kw_pos[None] >= 0)
    s_w = jnp.einsum('btgrd,bkgd->bgrtk', qg, kw) * scale
    p_w = masked_softmax(s_w, m_w)
    o_w = jnp.einsum('bgrtk,bkgd->btgrd', p_w.astype(vw.dtype), vw)
    gate = jax.nn.sigmoid(g_logit.astype(jnp.float32)).reshape(b, tq, N_KV, GROUP_R, 3)
    o = gate[..., 0:1] * o_c + gate[..., 1:2] * o_s + gate[..., 2:3] * o_w
    return o.reshape(b, tq, N_HEADS, HEAD_DIM)


def nsa_project(h, pos, w_in):
    p = h @ w_in
    b, t = p.shape[:2]
    sizes = [ATTN_WIDTH] + [KV_WIDTH] * 6 + [3 * N_HEADS]
    cuts = np.cumsum(sizes).tolist()
    parts = jnp.split(p, cuts, axis=-1)
    q = parts[0].reshape(b, t, N_HEADS, HEAD_DIM)
    kc, vc, ks, vs, kw, vw = [a.reshape(b, t, N_KV, HEAD_DIM) for a in parts[1:7]]
    g = parts[7].reshape(b, t, N_HEADS, 3)
    z = parts[8]
    return rope(q, pos), kc, vc, rope(ks, pos), vs, rope(kw, pos), vw, g, z


def nsa_output(o, z, w_out):
    b, t = o.shape[:2]
    o = o.reshape(b, t, ATTN_WIDTH).astype(z.dtype)
    return (o * jax.nn.silu(z)) @ w_out


def nsa_prompt(h, w_in, pe, wk1, wk2, wv1, wv2, w_out):
    b, s, _ = h.shape
    pos = jnp.arange(s)
    q, kc_r, vc_r, ks, vs, kw, vw, g, z = nsa_project(h, pos, w_in)
    kc = compress(kc_r, pe, wk1, wk2)
    vc = compress(vc_r, pe, wv1, wv2)
    ksb, vsb = sel_blocks(ks), sel_blocks(vs)
    pad = ((0, 0), (WINDOW, 0), (0, 0), (0, 0))
    kw_pad, vw_pad = jnp.pad(kw, pad), jnp.pad(vw, pad)

    def block(i):
        st = i * Q_BLOCK
        qb = lax.dynamic_slice_in_dim(q, st, Q_BLOCK, axis=1)
        gb = lax.dynamic_slice_in_dim(g, st, Q_BLOCK, axis=1)
        kwb = lax.dynamic_slice_in_dim(kw_pad, st, WINDOW + Q_BLOCK, axis=1)
        vwb = lax.dynamic_slice_in_dim(vw_pad, st, WINDOW + Q_BLOCK, axis=1)
        q_pos = st + jnp.arange(Q_BLOCK)
        kw_pos = st - WINDOW + jnp.arange(WINDOW + Q_BLOCK)
        return nsa_core(qb, q_pos, gb, kc, vc, ksb, vsb, kwb, vwb, kw_pos)

    o = lax.map(block, jnp.arange(s // Q_BLOCK))
    o = o.transpose(1, 0, 2, 3, 4).reshape(b, s, N_HEADS, HEAD_DIM)
    wl = min(WINDOW, s)
    return nsa_output(o, z, w_out), (kc_r, vc_r, ks, vs, kw[:, s - wl:], vw[:, s - wl:])


def nsa_sample(h, pool_kc, pool_vc, pool_ks, pool_vs, win_k, win_v, page_table, w_in, pe, wk1, wk2, wv1, wv2, w_out):
    b, t, _ = h.shape
    past = page_table.shape[1] * PAGE_SIZE
    pos = past + jnp.arange(t)
    q, kc_r, vc_r, ks, vs, kw, vw, g, z = nsa_project(h, pos, w_in)

    def full(pool, new):
        old = pool[page_table].reshape(b, past, N_KV, HEAD_DIM)
        return jnp.concatenate([old, new], axis=1)

    kc = compress(full(pool_kc, kc_r), pe, wk1, wk2)
    vc = compress(full(pool_vc, vc_r), pe, wv1, wv2)
    ksb, vsb = sel_blocks(full(pool_ks, ks)), sel_blocks(full(pool_vs, vs))
    wb = win_k.shape[1]
    kw_all = jnp.concatenate([win_k, kw], axis=1)
    vw_all = jnp.concatenate([win_v, vw], axis=1)
    kw_pos = past - wb + jnp.arange(wb + t)
    o = nsa_core(q, pos, g, kc, vc, ksb, vsb, kw_all, vw_all, kw_pos)
    return nsa_output(o, z, w_out), (kc_r, vc_r, ks, vs, kw_all[:, t:], vw_all[:, t:])


def s5_combine(e1, e2):
    a1r, a1i, b1r, b1i = e1
    a2r, a2i, b2r, b2i = e2
    return (a1r * a2r - a1i * a2i, a1r * a2i + a1i * a2r,
            a2r * b1r - a2i * b1i + b2r, a2r * b1i + a2i * b1r + b2i)


def s5_mixer(h, h0_re, h0_im, w_in, a_re, a_im, log_dt, b_re, b_im, c_re, c_im, d, w_glu, w_out):
    bsz, t, _ = h.shape
    u, z = jnp.split(h @ w_in, 2, axis=-1)
    f32 = jnp.float32
    ar, ai = a_re.astype(f32), a_im.astype(f32)
    dt = jnp.exp(log_dt.astype(f32))[:, None]
    mag = jnp.exp(dt * ar)
    lam_re, lam_im = mag * jnp.cos(dt * ai), mag * jnp.sin(dt * ai)
    den = ar * ar + ai * ai
    nr, ni = lam_re - 1.0, lam_im
    coef_re, coef_im = (nr * ar + ni * ai) / den, (ni * ar - nr * ai) / den
    uf = u.astype(f32).reshape(bsz, t, S5_NGROUPS, S5_GROUP)
    bu_re = jnp.einsum('btgc,gpc->btgp', uf, b_re.astype(f32))
    bu_im = jnp.einsum('btgc,gpc->btgp', uf, b_im.astype(f32))
    bb_re = coef_re * bu_re - coef_im * bu_im
    bb_im = coef_re * bu_im + coef_im * bu_re
    shp = (1, t, S5_NGROUPS, S5_STATE)
    acum_re, acum_im, hs_re, hs_im = lax.associative_scan(
        s5_combine, (jnp.broadcast_to(lam_re, shp), jnp.broadcast_to(lam_im, shp), bb_re, bb_im), axis=1)
    if h0_re is not None:
        p_re, p_im = h0_re.astype(f32)[:, None], h0_im.astype(f32)[:, None]
        hs_re, hs_im = (hs_re + acum_re * p_re - acum_im * p_im,
                        hs_im + acum_re * p_im + acum_im * p_re)
    y = (jnp.einsum('btgp,gcp->btgc', hs_re, c_re.astype(f32))
         - jnp.einsum('btgp,gcp->btgc', hs_im, c_im.astype(f32))
         + d.astype(f32).reshape(S5_NGROUPS, S5_GROUP) * uf)
    y = jax.nn.glu(y.reshape(bsz, t, S5_WIDTH) @ w_glu.astype(f32), axis=-1)
    out = (y * jax.nn.silu(z.astype(f32))).astype(h.dtype) @ w_out
    return out, hs_re[:, -1], hs_im[:, -1]


def setup_inputs(seed: int = 0) -> dict:
    key = jax.random.key(seed)
    ks = iter(jax.random.split(key, 48))
    nrm = lambda shape, s: jax.random.normal(next(ks), shape, jnp.float32) * s
    n_pages = PAST_LEN // PAGE_SIZE
    n_phys = (DEC_BATCH * n_pages * 5) // 4
    wbuf = min(WINDOW, PAST_LEN)
    NA, NB = N_ATTN_LAYERS, N_SSM_LAYERS
    pool = (NA, n_phys, PAGE_SIZE, N_KV, HEAD_DIM)
    win = (NA, DEC_BATCH, wbuf, N_KV, HEAD_DIM)
    st = (NB, DEC_BATCH, S5_NGROUPS, S5_STATE)
    d = {}
    d['x_prompt'] = nrm((BATCH, SEQ, D_MODEL), 1.0)
    d['x_sample'] = nrm((DEC_BATCH, DEC_SEQ, D_MODEL), 1.0)
    d['cache_k_cmp'] = nrm(pool, 1.0)
    d['cache_v_cmp'] = nrm(pool, 1.0)
    d['cache_k_sel'] = nrm(pool, 1.0)
    d['cache_v_sel'] = nrm(pool, 1.0)
    d['cache_k_win'] = nrm(win, 1.0)
    d['cache_v_win'] = nrm(win, 1.0)
    d['state_s5_re'] = nrm(st, 0.1)
    d['state_s5_im'] = nrm(st, 0.1)
    perm = jax.random.permutation(next(ks), n_phys)[:DEC_BATCH * n_pages]
    d['page_table'] = perm.reshape(DEC_BATCH, n_pages).astype(jnp.int32)
    d['c_prompt'] = nrm((BATCH, D_MODEL), 1.0)
    d['c_sample'] = nrm((DEC_BATCH, D_MODEL), 1.0)
    d['norm_pre'] = 1.0 + nrm((DEPTH, D_MODEL), 0.05)
    d['norm_post'] = 1.0 + nrm((DEPTH, D_MODEL), 0.05)
    d['w_ada'] = nrm((DEPTH, D_MODEL, 3 * D_MODEL), 0.1 * D_MODEL ** -0.5)
    d['b_ada'] = nrm((DEPTH, 3 * D_MODEL), 0.01)
    d['w_in_attn'] = nrm((NA, D_MODEL, ATTN_IN), D_MODEL ** -0.5)
    d['pe_cmp'] = nrm((NA, CMP_LEN, HEAD_DIM), 0.1)
    d['w_cmp_k1'] = nrm((NA, CMP_LEN * HEAD_DIM, CMP_HIDDEN), (CMP_LEN * HEAD_DIM) ** -0.5)
    d['w_cmp_k2'] = nrm((NA, CMP_HIDDEN, HEAD_DIM), CMP_HIDDEN ** -0.5)
    d['w_cmp_v1'] = nrm((NA, CMP_LEN * HEAD_DIM, CMP_HIDDEN), (CMP_LEN * HEAD_DIM) ** -0.5)
    d['w_cmp_v2'] = nrm((NA, CMP_HIDDEN, HEAD_DIM), CMP_HIDDEN ** -0.5)
    d['w_out_attn'] = nrm((NA, ATTN_WIDTH, D_MODEL), ATTN_WIDTH ** -0.5)
    d['w_in_ssm'] = nrm((NB, D_MODEL, 2 * S5_WIDTH), D_MODEL ** -0.5)
    d['s5_a_re'] = -0.5 + nrm((NB, S5_NGROUPS, S5_STATE), 0.01)
    d['s5_a_im'] = math.pi * jnp.arange(S5_STATE, dtype=jnp.float32) + nrm((NB, S5_NGROUPS, S5_STATE), 0.01)
    d['s5_log_dt'] = jax.random.uniform(next(ks), (NB, S5_NGROUPS), jnp.float32, math.log(1e-3), math.log(1e-1))
    bs = (S5_GROUP * 2) ** -0.5
    cs = (S5_STATE * 2) ** -0.5
    d['s5_b_re'] = nrm((NB, S5_NGROUPS, S5_STATE, S5_GROUP), bs)
    d['s5_b_im'] = nrm((NB, S5_NGROUPS, S5_STATE, S5_GROUP), bs)
    d['s5_c_re'] = nrm((NB, S5_NGROUPS, S5_GROUP, S5_STATE), cs)
    d['s5_c_im'] = nrm((NB, S5_NGROUPS, S5_GROUP, S5_STATE), cs)
    d['s5_d'] = nrm((NB, S5_WIDTH), 0.5)
    d['w_glu'] = nrm((NB, S5_WIDTH, 2 * S5_WIDTH), S5_WIDTH ** -0.5)
    d['w_out_ssm'] = nrm((NB, S5_WIDTH, D_MODEL), S5_WIDTH ** -0.5)
    return d


def reference(x_prompt, x_sample, cache_k_cmp, cache_v_cmp, cache_k_sel, cache_v_sel, cache_k_win, cache_v_win,
              state_s5_re, state_s5_im, page_table, c_prompt, c_sample, norm_pre, norm_post, w_ada, b_ada,
              w_in_attn, pe_cmp, w_cmp_k1, w_cmp_k2, w_cmp_v1, w_cmp_v2, w_out_attn,
              w_in_ssm, s5_a_re, s5_a_im, s5_log_dt, s5_b_re, s5_b_im, s5_c_re, s5_c_im, s5_d, w_glu, w_out_ssm):
    yp, ys = x_prompt, x_sample
    attn_p = [[] for _ in range(6)]
    attn_s = [[] for _ in range(6)]
    ssm_p = [[], []]
    ssm_s = [[], []]
    for i in range(DEPTH):
        hp, gp = modulate(yp, c_prompt, norm_pre[i], w_ada[i], b_ada[i])
        hs, gs = modulate(ys, c_sample, norm_pre[i], w_ada[i], b_ada[i])
        li = i // N_MIXERS
        if i % N_MIXERS == 0:
            wts = (w_in_attn[li], pe_cmp[li], w_cmp_k1[li], w_cmp_k2[li], w_cmp_v1[li], w_cmp_v2[li], w_out_attn[li])
            op, rows_p = nsa_prompt(hp, *wts)
            os_, rows_s = nsa_sample(hs, cache_k_cmp[li], cache_v_cmp[li], cache_k_sel[li], cache_v_sel[li],
                                     cache_k_win[li], cache_v_win[li], page_table, *wts)
            for n in range(6):
                attn_p[n].append(rows_p[n])
                attn_s[n].append(rows_s[n])
        else:
            wts = (w_in_ssm[li], s5_a_re[li], s5_a_im[li], s5_log_dt[li], s5_b_re[li], s5_b_im[li],
                   s5_c_re[li], s5_c_im[li], s5_d[li], w_glu[li], w_out_ssm[li])
            op, pr, pi = s5_mixer(hp, None, None, *wts)
            os_, sr, si = s5_mixer(hs, state_s5_re[li], state_s5_im[li], *wts)
            ssm_p[0].append(pr)
            ssm_p[1].append(pi)
            ssm_s[0].append(sr)
            ssm_s[1].append(si)
        yp = yp + gp[:, None] * rms_norm(op, norm_post[i])
        ys = ys + gs[:, None] * rms_norm(os_, norm_post[i])
    new_k_cmp_prompt, new_k_cmp_sample = jnp.stack(attn_p[0]), jnp.stack(attn_s[0])
    new_v_cmp_prompt, new_v_cmp_sample = jnp.stack(attn_p[1]), jnp.stack(attn_s[1])
    new_k_sel_prompt, new_k_sel_sample = jnp.stack(attn_p[2]), jnp.stack(attn_s[2])
    new_v_sel_prompt, new_v_sel_sample = jnp.stack(attn_p[3]), jnp.stack(attn_s[3])
    new_k_win_prompt, new_k_win_sample = jnp.stack(attn_p[4]), jnp.stack(attn_s[4])
    new_v_win_prompt, new_v_win_sample = jnp.stack(attn_p[5]), jnp.stack(attn_s[5])
    new_s5_re_prompt, new_s5_re_sample = jnp.stack(ssm_p[0]), jnp.stack(ssm_s[0])
    new_s5_im_prompt, new_s5_im_sample = jnp.stack(ssm_p[1]), jnp.stack(ssm_s[1])
    return (yp, ys,
            new_k_cmp_prompt, new_k_cmp_sample, new_v_cmp_prompt, new_v_cmp_sample,
            new_k_sel_prompt, new_k_sel_sample, new_v_sel_prompt, new_v_sel_sample,
            new_k_win_prompt, new_k_win_sample, new_v_win_prompt, new_v_win_sample,
            new_s5_re_prompt, new_s5_re_sample, new_s5_im_prompt, new_s5_im_sample)
```

```python
import functools
import math

import jax
import jax.numpy as jnp
from jax import lax
from jax.experimental import pallas as pl
from jax.experimental.pallas import tpu as pltpu

N_HEADS = 16
HEAD_DIM = 64
N_KV = 4
GROUP_R = N_HEADS // N_KV
ROT_HALF = HEAD_DIM // 8
ROPE_THETA = 500000.0
CMP_LEN = 32
CMP_STRIDE = 16
CMP_RATIO = CMP_LEN // CMP_STRIDE
CMP_HIDDEN = 64
SEL_LEN = 64
SEL_SHIFT = SEL_LEN.bit_length() - 1
N_SEL = 16
WINDOW = 512
PAGE_SIZE = 128
S5_GROUP = 16
S5_STATE = 64
NEG = -1e30
SEL_FORCE = 1e4
RMS_EPS = 1e-6
KV_WIDTH = N_KV * HEAD_DIM

LANES = 128
SUBLANES = 8
VMEM_LIMIT_BYTES = 56 * 1024 * 1024

MXU_DT = jnp.bfloat16
F32 = jnp.float32

KEY_TILE = 256
BLOCKS_PER_TILE = KEY_TILE // SEL_LEN
PAGES_PER_STEP = 32
S5_QUARTERS = 4


def _cparams(sem):
    return pltpu.CompilerParams(dimension_semantics=sem, vmem_limit_bytes=VMEM_LIMIT_BYTES)


def _dot(a, b):
    return jnp.dot(a.astype(MXU_DT), b.astype(MXU_DT), preferred_element_type=F32)


def _dot_nt(a, b):
    return lax.dot_general(a.astype(MXU_DT), b.astype(MXU_DT), (((1,), (1,)), ((), ())),
                           preferred_element_type=F32)


def _split3(x):
    hi = x.astype(MXU_DT)
    r1 = x - hi.astype(F32)
    mid = r1.astype(MXU_DT)
    lo = (r1 - mid.astype(F32)).astype(MXU_DT)
    return hi, mid, lo


def _dot_exact_rhs(x, m):
    hi, mid, lo = _split3(x)
    return (jnp.dot(hi, m, preferred_element_type=F32) + jnp.dot(mid, m, preferred_element_type=F32)
            + jnp.dot(lo, m, preferred_element_type=F32))


def _dot_f32(a, b):
    ah, am, _ = _split3(a)
    bh, bm, _ = _split3(b)
    d = lambda p, q: jnp.dot(p, q, preferred_element_type=F32)
    return d(ah, bh) + (d(ah, bm) + d(am, bh)) + d(am, bm)


def _sigmoid(x):
    return 1.0 / (1.0 + jnp.exp(-x))


def _silu(x):
    return x * _sigmoid(x)


def _ada_kernel(c_ref, w_ref, b_ref, o_ref):
    o_ref[...] = _dot_f32(c_ref[...], w_ref[...]) + b_ref[...]


def _ada(c_all, w_ada, b_ada):
    depth, d, d3 = w_ada.shape
    n = c_all.shape[0]
    return pl.pallas_call(
        _ada_kernel,
        grid=(depth, d3 // d),
        in_specs=[pl.BlockSpec((n, d), lambda i, j: (0, 0)),
                  pl.BlockSpec((None, d, d), lambda i, j: (i, 0, j)),
                  pl.BlockSpec((None, 1, d), lambda i, j: (i, 0, j))],
        out_specs=pl.BlockSpec((None, n, d), lambda i, j: (i, 0, j)),
        out_shape=jax.ShapeDtypeStruct((depth, n, d3), F32),
        compiler_params=_cparams(("arbitrary", "arbitrary")),
        name="ada_mod",
    )(c_all, w_ada, b_ada.reshape(depth, 1, d3))


def _rms_mod(x, g, scale, shift):
    y = x * lax.rsqrt(jnp.mean(x * x, axis=-1, keepdims=True) + RMS_EPS)
    return (y * g) * (1.0 + scale) + shift


def _rope_slab(x, c, s1, s2):
    return x * c + pltpu.roll(x, ROT_HALF, 1) * s1 + pltpu.roll(x, LANES - ROT_HALF, 1) * s2


def _attn_proj_kernel(x_ref, shift_ref, scale_ref, g_ref, c_ref, s1_ref, s2_ref,
                      wq_ref, wkv_ref, wg_ref, wz_ref,
                      q_ref, kc_ref, vc_ref, ks_ref, vs_ref, kw_ref, vw_ref,
                      ksb_ref, vsb_ref, kwb_ref, vwb_ref, gl_ref, z_ref):
    h = _rms_mod(x_ref[0], g_ref[...], scale_ref[0], shift_ref[0]).astype(MXU_DT)
    c, s1, s2 = c_ref[...], s1_ref[...], s2_ref[...]
    q = jnp.dot(h, wq_ref[...], preferred_element_type=F32)
    qscale = HEAD_DIM ** -0.5
    for j in range(q.shape[1] // LANES):
        sl = slice(j * LANES, (j + 1) * LANES)
        q_ref[0, :, sl] = (_rope_slab(q[:, sl], c, s1, s2) * qscale).astype(q_ref.dtype)
    kv = jnp.dot(h, wkv_ref[...], preferred_element_type=F32)
    outs = ((kc_ref, None, False), (vc_ref, None, False), (ks_ref, ksb_ref, True),
            (vs_ref, vsb_ref, False), (kw_ref, kwb_ref, True), (vw_ref, vwb_ref, False))
    for n, (o_ref, ob_ref, roped) in enumerate(outs):
        for j in range(KV_WIDTH // LANES):
            lo = n * KV_WIDTH + j * LANES
            v = kv[:, lo:lo + LANES]
            if roped:
                v = _rope_slab(v, c, s1, s2)
            o_ref[0, :, j * LANES:(j + 1) * LANES] = v
            if ob_ref is not None:
                ob_ref[0, :, j * LANES:(j + 1) * LANES] = v.astype(ob_ref.dtype)
    gl_ref[0] = jnp.dot(h, wg_ref[...], preferred_element_type=F32)
    z_ref[0] = jnp.dot(h, wz_ref[...], preferred_element_type=F32)


def _attn_proj(x, shift, scale, g_pre, tables, wq, wkv, wg, wz):
    b, t, d = x.shape
    tm = min(t, 256)
    cos_t, s1_t, s2_t = tables
    row = lambda i, j: (i, j, 0)
    per_b = pl.BlockSpec((1, 1, d), lambda i, j: (i, 0, 0))
    full = lambda a: pl.BlockSpec(a.shape, lambda i, j: (0,) * a.ndim)
    tab = pl.BlockSpec((tm, LANES), lambda i, j: (j, 0))
    kv32 = jax.ShapeDtypeStruct((b, t, KV_WIDTH), F32)
    kv16 = jax.ShapeDtypeStruct((b, t, KV_WIDTH), MXU_DT)
    kvspec = pl.BlockSpec((1, tm, KV_WIDTH), row)
    return pl.pallas_call(
        _attn_proj_kernel,
        grid=(b, t // tm),
        in_specs=[pl.BlockSpec((1, tm, d), row), per_b, per_b, full(g_pre), tab, tab, tab,
                  full(wq), full(wkv), full(wg), full(wz)],
        out_specs=[pl.BlockSpec((1, tm, d), row)] + [kvspec] * 10
                  + [pl.BlockSpec((1, tm, LANES), row), pl.BlockSpec((1, tm, d), row)],
        out_shape=[jax.ShapeDtypeStruct((b, t, d), MXU_DT)] + [kv32] * 6 + [kv16] * 4
                  + [jax.ShapeDtypeStruct((b, t, LANES), F32), jax.ShapeDtypeStruct((b, t, d), F32)],
        compiler_params=_cparams(("arbitrary", "arbitrary")),
        name="attn_proj",
    )(x, shift, scale, g_pre, cos_t, s1_t, s2_t, wq, wkv, wg, wz)


def _rope_tables(pos):
    inv = ROPE_THETA ** (-jnp.arange(ROT_HALF, dtype=F32) / ROT_HALF)
    ang = pos.astype(F32)[:, None] * inv[None]
    cos, sin = jnp.cos(ang), jnp.sin(ang)
    n = pos.shape[0]
    rest = HEAD_DIM - 2 * ROT_HALF
    zeros = lambda w: jnp.zeros((n, w), F32)
    c64 = jnp.concatenate([cos, cos, jnp.ones((n, rest), F32)], axis=1)
    s1 = jnp.concatenate([zeros(ROT_HALF), sin, zeros(rest)], axis=1)
    s2 = jnp.concatenate([-sin, zeros(ROT_HALF + rest)], axis=1)
    rep = LANES // HEAD_DIM
    return tuple(jnp.tile(a, (1, rep)) for a in (c64, s1, s2))


def _cmp_part_kernel(xk_ref, xv_ref, wk_ref, wv_ref, pk_ref, pv_ref):
    pk_ref[...] = _dot(xk_ref[...], wk_ref[...])
    pv_ref[...] = _dot(xv_ref[...], wv_ref[...])


def _cmp_part(xk, xv, wk, wv):
    rows, width = xk.shape
    tm = min(rows, 256)
    n = wk.shape[1]
    xs = pl.BlockSpec((tm, width), lambda i: (i, 0))
    ws = pl.BlockSpec(wk.shape, lambda i: (0, 0))
    os_ = pl.BlockSpec((tm, n), lambda i: (i, 0))
    return pl.pallas_call(
        _cmp_part_kernel, grid=(rows // tm,), in_specs=[xs, xs, ws, ws], out_specs=[os_, os_],
        out_shape=[jax.ShapeDtypeStruct((rows, n), F32)] * 2,
        compiler_params=_cparams(("arbitrary",)), name="cmp_part",
    )(xk, xv, wk, wv)


def _page_copy(pool_ref, pt_ref, b, page, buf_ref, slot, sem):
    return pltpu.make_async_copy(pool_ref.at[pt_ref[b, page]], buf_ref.at[slot], sem)


def _gather_pages(pt_ref, b, first_page, pools, bufs, sems):
    def start(p, _):
        for pool, buf, k in zip(pools, bufs, range(len(pools))):
            _page_copy(pool, pt_ref, b, first_page + p, buf, p, sems.at[k]).start()
        return 0

    def wait(p, _):
        for pool, buf, k in zip(pools, bufs, range(len(pools))):
            _page_copy(pool, pt_ref, b, first_page + p, buf, p, sems.at[k]).wait()
        return 0

    lax.fori_loop(0, PAGES_PER_STEP, start, 0)
    lax.fori_loop(0, PAGES_PER_STEP, wait, 0)


def _cmp_part_paged_kernel(pt_ref, poolk_ref, poolv_ref, wk_ref, wv_ref, pk_ref, pv_ref,
                           bufk, bufv, sems):
    b, c = pl.program_id(0), pl.program_id(1)
    _gather_pages(pt_ref, b, c * PAGES_PER_STEP, (poolk_ref, poolv_ref), (bufk, bufv), sems)
    rows = bufk.shape[0] * bufk.shape[1]
    pk_ref[0] = _dot(bufk[...].reshape(rows, bufk.shape[2]), wk_ref[...])
    pv_ref[0] = _dot(bufv[...].reshape(rows, bufv.shape[2]), wv_ref[...])


def _cmp_part_paged(page_table, poolk, poolv, wk, wv):
    nb, n_pages = page_table.shape
    _, cpp, width = poolk.shape
    n = wk.shape[1]
    rows = PAGES_PER_STEP * cpp
    any_spec = pl.BlockSpec(memory_space=pl.ANY)
    ws = pl.BlockSpec(wk.shape, lambda b, c, pt: (0, 0))
    os_ = pl.BlockSpec((1, rows, n), lambda b, c, pt: (b, c, 0))
    grid_spec = pltpu.PrefetchScalarGridSpec(
        num_scalar_prefetch=1, grid=(nb, n_pages // PAGES_PER_STEP),
        in_specs=[any_spec, any_spec, ws, ws], out_specs=[os_, os_],
        scratch_shapes=[pltpu.VMEM((PAGES_PER_STEP, cpp, width), F32),
                        pltpu.VMEM((PAGES_PER_STEP, cpp, width), F32),
                        pltpu.SemaphoreType.DMA((2,))])
    return pl.pallas_call(
        _cmp_part_paged_kernel, grid_spec=grid_spec,
        out_shape=[jax.ShapeDtypeStruct((nb, n_pages * cpp, n), F32)] * 2,
        compiler_params=_cparams(("arbitrary", "arbitrary")), name="cmp_part_paged",
    )(page_table, poolk, poolv, wk, wv)


def _cmp_finish_kernel(pk_ref, pv_ref, pe_ref, w1k_ref, w1v_ref, w2k_ref, w2v_ref, kc_ref, vc_ref):
    pe = pe_ref[...]
    for p_ref, w1_ref, w2_ref, o_ref in ((pk_ref, w1k_ref, w2k_ref, kc_ref),
                                         (pv_ref, w1v_ref, w2v_ref, vc_ref)):
        p = p_ref[0]
        n_ch = p.shape[0]
        p0, p1 = p[:, :KV_WIDTH], p[:, KV_WIDTH:]
        row = lax.broadcasted_iota(jnp.int32, p1.shape, 0)
        p1_next = jnp.where(row == n_ch - 1, 0.0, pltpu.roll(p1, n_ch - 1, 0))
        bias = jnp.sum(pe * w1_ref[...], axis=0, keepdims=True)
        o_ref[0] = _dot(_silu(p0 + p1_next + bias), w2_ref[...]).astype(o_ref.dtype)


def _cmp_finish(pk, pv, pe_col, w1k_t, w1v_t, w2k_bd, w2v_bd):
    nb, n_ch, n = pk.shape
    ps = pl.BlockSpec((1, n_ch, n), lambda b: (b, 0, 0))
    full = lambda a: pl.BlockSpec(a.shape, lambda b: (0,) * a.ndim)
    os_ = pl.BlockSpec((1, n_ch, KV_WIDTH), lambda b: (b, 0, 0))
    return pl.pallas_call(
        _cmp_finish_kernel, grid=(nb,),
        in_specs=[ps, ps, full(pe_col), full(w1k_t), full(w1v_t), full(w2k_bd), full(w2v_bd)],
        out_specs=[os_, os_],
        out_shape=[jax.ShapeDtypeStruct((nb, n_ch, KV_WIDTH), MXU_DT)] * 2,
        compiler_params=_cparams(("arbitrary",)), name="cmp_finish",
    )(pk, pv, pe_col, w1k_t, w1v_t, w2k_bd, w2v_bd)


def _cmp_weights(w1, w2):
    eye = jnp.eye(N_KV, dtype=F32)
    w1r = w1.reshape(CMP_RATIO, CMP_STRIDE, HEAD_DIM, CMP_HIDDEN)
    big = jnp.einsum('rsdh,gk->sgdrkh', w1r, eye).reshape(CMP_STRIDE * KV_WIDTH, CMP_RATIO * KV_WIDTH)
    w1_t = jnp.tile(w1, (1, N_KV))
    w2_bd = jnp.einsum('hd,gk->ghkd', w2, eye).reshape(N_KV * CMP_HIDDEN, KV_WIDTH)
    return big.astype(MXU_DT), w1_t, w2_bd.astype(MXU_DT)


def _lane_half_mask(shape, half):
    lane = lax.broadcasted_iota(jnp.int32, shape, len(shape) - 1)
    return (lane >= HEAD_DIM) if half else (lane < HEAD_DIM)


def _build_q4(q_ref, g, tq):
    hg = g % 2
    parts = []
    for r in range(GROUP_R):
        h = g * GROUP_R + r
        slab = q_ref[0, :, (h // 2) * LANES:(h // 2 + 1) * LANES].astype(F32)
        if h % 2 != hg:
            slab = pltpu.roll(slab, HEAD_DIM, 1)
        parts.append(jnp.where(_lane_half_mask(slab.shape, hg), slab, 0.0))
    return jnp.concatenate(parts, axis=0).astype(MXU_DT)


def _row_qpos(rows, cols, tq, q0):
    t = lax.broadcasted_iota(jnp.int32, (rows, cols), 0) & (tq - 1)
    return q0 + t


def _softmax_attend(q4, k, v, mask):
    s = jnp.where(mask, _dot_nt(q4, k), NEG)
    m = jnp.max(s, axis=1, keepdims=True)
    e = jnp.where(mask, jnp.exp(s - m), 0.0)
    l = jnp.sum(e, axis=1, keepdims=True)
    inv = jnp.where(l > 0.0, 1.0 / l, 0.0)
    return _dot(e, v) * inv, e * inv


def _top_blocks(imp, qpos, n_blocks):
    j = lax.broadcasted_iota(jnp.int32, imp.shape, 1)
    cur = qpos >> SEL_SHIFT
    forced = (j == 0) | (j == cur) | (j == cur - 1)
    valid = j * SEL_LEN <= qpos
    v = jnp.where(valid, imp + jnp.where(forced, SEL_FORCE, 0.0), -SEL_FORCE)
    v = jnp.where(j < n_blocks, v, -jnp.inf)
    sel = jnp.zeros(imp.shape, F32)
    jf = j.astype(F32)
    for _ in range(N_SEL):
        m = jnp.max(v, axis=1, keepdims=True)
        first = jnp.min(jnp.where(v == m, jf, float(imp.shape[1])), axis=1, keepdims=True)
        hit = jf == first
        sel = jnp.where(hit, 1.0, sel)
        v = jnp.where(hit, -jnp.inf, v)
    return sel


def _compressed_and_select(q_ref, kc_ref, vc_ref, ov_ref, oc_ref, sel_ref, tq, q0, n_blocks):
    ncp = kc_ref.shape[1]
    rows = GROUP_R * tq
    n = lax.broadcasted_iota(jnp.int32, (rows, ncp), 1)
    mask_c = n * CMP_STRIDE + (CMP_LEN - 1) <= _row_qpos(rows, ncp, tq, q0)
    psums = []
    for g in range(N_KV):
        sl = slice((g // 2) * LANES, (g // 2 + 1) * LANES)
        o, p = _softmax_attend(_build_q4(q_ref, g, tq), kc_ref[0, :, sl], vc_ref[0, :, sl], mask_c)
        oc_ref[g] = o
        ps = p[0:tq]
        for r in range(1, GROUP_R):
            ps = ps + p[r * tq:(r + 1) * tq]
        psums.append(ps)
    imp = _dot_exact_rhs(jnp.concatenate(psums, axis=0), ov_ref[...])
    qpos = _row_qpos(N_KV * tq, imp.shape[1], tq, q0)
    sel_ref[...] = _top_blocks(imp, qpos, n_blocks).astype(sel_ref.dtype)


def _selected_tile_update(q4s, k, v, sel, first_block, key0, tq, q0, m_ref, l_ref, acc_ref):
    nb = sel.shape[1]
    jj = lax.broadcasted_iota(jnp.int32, (nb, KEY_TILE), 0)
    kk = lax.broadcasted_iota(jnp.int32, (nb, KEY_TILE), 1)
    expand = jnp.where(jj == first_block + (kk >> SEL_SHIFT), 1.0, 0.0).astype(MXU_DT)
    sel_keys = jnp.dot(sel, expand, preferred_element_type=F32)
    rows = GROUP_R * tq
    kpos = key0 + lax.broadcasted_iota(jnp.int32, (rows, KEY_TILE), 1)
    causal = kpos <= _row_qpos(rows, KEY_TILE, tq, q0)
    for g in range(N_KV):
        sl = slice((g // 2) * LANES, (g // 2 + 1) * LANES)
        sel_rows = jnp.concatenate([sel_keys[g * tq:(g + 1) * tq]] * GROUP_R, axis=0)
        mask = (sel_rows > 0.5) & causal
        s = jnp.where(mask, _dot_nt(q4s[g], k[:, sl]), NEG)
        m_old = m_ref[g]
        m_new = jnp.maximum(m_old, jnp.max(s, axis=1, keepdims=True))
        alpha = jnp.exp(m_old - m_new)
        e = jnp.where(mask, jnp.exp(s - m_new[:, :1]), 0.0)
        l_ref[g] = alpha * l_ref[g] + jnp.sum(e, axis=1, keepdims=True)
        acc_ref[g] = alpha * acc_ref[g] + _dot(e, v[:, sl])
        m_ref[g] = m_new


def _init_softmax_state(m_ref, l_ref, acc_ref):
    m_ref[...] = jnp.full(m_ref.shape, NEG, F32)
    l_ref[...] = jnp.zeros(l_ref.shape, F32)
    acc_ref[...] = jnp.zeros(acc_ref.shape, F32)


def _combine_and_store(gl_ref, oc_ref, ow_ref, l_ref, acc_ref, o_ref, tq):
    gates = _sigmoid(gl_ref[0])
    for g in range(N_KV):
        hg = g % 2
        l = l_ref[g]
        o_s = acc_ref[g] * jnp.where(l > 0.0, 1.0 / l, 0.0)
        o_c, o_w = oc_ref[g], ow_ref[g]
        heads = []
        for r in range(GROUP_R):
            h = g * GROUP_R + r
            rs = slice(r * tq, (r + 1) * tq)
            gc = [gates[:, 3 * h + j:3 * h + j + 1] for j in range(3)]
            x = gc[0] * o_c[rs] + gc[1] * o_s[rs] + gc[2] * o_w[rs]
            if h % 2 != hg:
                x = pltpu.roll(x, HEAD_DIM, 1)
            heads.append(x)
        for i in range(GROUP_R // 2):
            slab = jnp.where(_lane_half_mask(heads[0].shape, 0), heads[2 * i], heads[2 * i + 1])
            lo = (g * GROUP_R // 2 + i) * LANES
            o_ref[0, :, lo:lo + LANES] = slab


def _window_attend(q_ref, kw, vw, mask, ow_ref, tq):
    for g in range(N_KV):
        sl = slice((g // 2) * LANES, (g // 2 + 1) * LANES)
        o, _ = _softmax_attend(_build_q4(q_ref, g, tq), kw[:, sl], vw[:, sl], mask)
        ow_ref[g] = o


def _attn_prompt_kernel(*refs, tq, n_win):
    (q_ref, gl_ref, kc_ref, vc_ref, ov_ref, ks_ref, vs_ref) = refs[:7]
    kw_refs = refs[7:7 + n_win]
    vw_refs = refs[7 + n_win:7 + 2 * n_win]
    o_ref = refs[7 + 2 * n_win]
    oc_ref, ow_ref, sel_ref, m_ref, l_ref, acc_ref = refs[8 + 2 * n_win:]
    i = pl.program_id(1)
    q0 = i * tq
    n_blocks = ks_ref.shape[1] // SEL_LEN
    rows = GROUP_R * tq

    _compressed_and_select(q_ref, kc_ref, vc_ref, ov_ref, oc_ref, sel_ref, tq, q0, n_blocks)

    kw = jnp.concatenate([r[0] for r in kw_refs], axis=0)
    vw = jnp.concatenate([r[0] for r in vw_refs], axis=0)
    lw = kw.shape[0]
    kpos = (i - (n_win - 1)) * tq + lax.broadcasted_iota(jnp.int32, (rows, lw), 1)
    qpos = _row_qpos(rows, lw, tq, q0)
    _window_attend(q_ref, kw, vw, (kpos <= qpos) & (kpos >= qpos - WINDOW) & (kpos >= 0), ow_ref, tq)

    _init_softmax_state(m_ref, l_ref, acc_ref)
    q4s = [_build_q4(q_ref, g, tq) for g in range(N_KV)]
    sel = sel_ref[...]

    def body(kt, carry):
        k0 = pl.multiple_of(kt * KEY_TILE, KEY_TILE)
        _selected_tile_update(q4s, ks_ref[0, pl.ds(k0, KEY_TILE), :], vs_ref[0, pl.ds(k0, KEY_TILE), :],
                              sel, kt * BLOCKS_PER_TILE, k0, tq, q0, m_ref, l_ref, acc_ref)
        return carry

    lax.fori_loop(0, (q0 + tq + KEY_TILE - 1) // KEY_TILE, body, 0)
    _combine_and_store(gl_ref, oc_ref, ow_ref, l_ref, acc_ref, o_ref, tq)


def _attn_prompt(q, glog, kc, vc, ov, ks, vs, kw, vw):
    b, t, d = q.shape
    tq = 128
    n_win = WINDOW // tq + 1
    ncp = kc.shape[1]
    nbp = ov.shape[1]
    blk = lambda w: pl.BlockSpec((1, tq, w), lambda bi, i: (bi, i, 0))
    per_b = lambda a: pl.BlockSpec((1,) + a.shape[1:], lambda bi, i: (bi, 0, 0))
    win = [pl.BlockSpec((1, tq, KV_WIDTH),
                        functools.partial(lambda bi, i, k: (bi, jnp.maximum(i - (n_win - 1) + k, 0), 0), k=k))
           for k in range(n_win)]
    rows = GROUP_R * tq
    return pl.pallas_call(
        functools.partial(_attn_prompt_kernel, tq=tq, n_win=n_win),
        grid=(b, t // tq),
        in_specs=[blk(d), blk(LANES), per_b(kc), per_b(vc), pl.BlockSpec(ov.shape, lambda bi, i: (0, 0)),
                  per_b(ks), per_b(vs)] + win + win,
        out_specs=blk(d),
        out_shape=jax.ShapeDtypeStruct((b, t, d), F32),
        scratch_shapes=[pltpu.VMEM((N_KV, rows, LANES), F32), pltpu.VMEM((N_KV, rows, LANES), F32),
                        pltpu.VMEM((N_KV * tq, nbp), MXU_DT),
                        pltpu.VMEM((N_KV, rows, LANES), F32), pltpu.VMEM((N_KV, rows, LANES), F32),
                        pltpu.VMEM((N_KV, rows, LANES), F32)],
        compiler_params=_cparams(("arbitrary", "arbitrary")), name="attn_prompt",
    )(q, glog, kc, vc, ov, ks, vs, *([kw] * n_win), *([vw] * n_win))


def _attn_sample_kernel(pt_ref, q_ref, gl_ref, kc_ref, vc_ref, ov_ref, kw_ref, vw_ref, kn_ref, vn_ref,
                        poolk_ref, poolv_ref, o_ref,
                        bufk, bufv, sems, oc_ref, ow_ref, sel_ref, m_ref, l_ref, acc_ref,
                        *, tq, past, n_chunks):
    b, c = pl.program_id(0), pl.program_id(1)
    q0 = past
    rows = GROUP_R * tq
    n_blocks = past // SEL_LEN + 1

    @pl.when(c == 0)
    def _():
        _compressed_and_select(q_ref, kc_ref, vc_ref, ov_ref, oc_ref, sel_ref, tq, q0, n_blocks)
        lw = kw_ref.shape[1]
        kpos = past - WINDOW + lax.broadcasted_iota(jnp.int32, (rows, lw), 1)
        qpos = _row_qpos(rows, lw, tq, q0)
        _window_attend(q_ref, kw_ref[0], vw_ref[0], (kpos <= qpos) & (kpos >= qpos - WINDOW), ow_ref, tq)
        _init_softmax_state(m_ref, l_ref, acc_ref)

    q4s = [_build_q4(q_ref, g, tq) for g in range(N_KV)]
    sel = sel_ref[...]

    @pl.when(c < n_chunks)
    def _():
        _gather_pages(pt_ref, b, c * PAGES_PER_STEP, (poolk_ref, poolv_ref), (bufk, bufv), sems)
        pages_per_tile = KEY_TILE // PAGE_SIZE

        def body(kt, carry):
            p0 = kt * pages_per_tile
            k = bufk[pl.ds(p0, pages_per_tile)].reshape(KEY_TILE, KV_WIDTH)
            v = bufv[pl.ds(p0, pages_per_tile)].reshape(KEY_TILE, KV_WIDTH)
            page0 = c * PAGES_PER_STEP + p0
            _selected_tile_update(q4s, k, v, sel, page0 * (PAGE_SIZE // SEL_LEN), page0 * PAGE_SIZE,
                                  tq, q0, m_ref, l_ref, acc_ref)
            return carry

        lax.fori_loop(0, PAGES_PER_STEP // pages_per_tile, body, 0)

    @pl.when(c == n_chunks)
    def _():
        _selected_tile_update(q4s, kn_ref[0], vn_ref[0], sel, past // SEL_LEN, past, tq, q0,
                              m_ref, l_ref, acc_ref)
        _combine_and_store(gl_ref, oc_ref, ow_ref, l_ref, acc_ref, o_ref, tq)


def _attn_sample(page_table, q, glog, kc, vc, ov, kw_full, vw_full, k_new, v_new, poolk, poolv):
    b, tq, d = q.shape
    n_pages = page_table.shape[1]
    past = n_pages * PAGE_SIZE
    n_chunks = n_pages // PAGES_PER_STEP
    rows = GROUP_R * tq
    nbp = ov.shape[1]
    per_b = lambda a: pl.BlockSpec((1,) + a.shape[1:], lambda bi, c, pt: (bi, 0, 0))
    any_spec = pl.BlockSpec(memory_space=pl.ANY)
    grid_spec = pltpu.PrefetchScalarGridSpec(
        num_scalar_prefetch=1, grid=(b, n_chunks + 1),
        in_specs=[per_b(q), per_b(glog), per_b(kc), per_b(vc),
                  pl.BlockSpec(ov.shape, lambda bi, c, pt: (0, 0)),
                  per_b(kw_full), per_b(vw_full), per_b(k_new), per_b(v_new), any_spec, any_spec],
        out_specs=per_b(q),
        scratch_shapes=[pltpu.VMEM((PAGES_PER_STEP, PAGE_SIZE, KV_WIDTH), F32),
                        pltpu.VMEM((PAGES_PER_STEP, PAGE_SIZE, KV_WIDTH), F32),
                        pltpu.SemaphoreType.DMA((2,)),
                        pltpu.VMEM((N_KV, rows, LANES), F32), pltpu.VMEM((N_KV, rows, LANES), F32),
                        pltpu.VMEM((N_KV * tq, nbp), MXU_DT),
                        pltpu.VMEM((N_KV, rows, LANES), F32), pltpu.VMEM((N_KV, rows, LANES), F32),
                        pltpu.VMEM((N_KV, rows, LANES), F32)])
    return pl.pallas_call(
        functools.partial(_attn_sample_kernel, tq=tq, past=past, n_chunks=n_chunks),
        grid_spec=grid_spec,
        out_shape=jax.ShapeDtypeStruct((b, tq, d), F32),
        compiler_params=_cparams(("arbitrary", "arbitrary")), name="attn_sample",
    )(page_table, q, glog, kc, vc, ov, kw_full, vw_full, k_new, v_new, poolk, poolv)


def _win_update_kernel(ck_ref, cv_ref, nk_ref, nv_ref, ok_ref, ov_ref, fk_ref, fv_ref):
    t = nk_ref.shape[1]
    wb = ck_ref.shape[1]
    for c_ref, n_ref, o_ref, f_ref in ((ck_ref, nk_ref, ok_ref, fk_ref), (cv_ref, nv_ref, ov_ref, fv_ref)):
        o_ref[0, 0:wb - t, :] = c_ref[0, t:wb, :]
        o_ref[0, wb - t:wb, :] = n_ref[0]
        pad = jnp.zeros((f_ref.shape[1] - wb - t, f_ref.shape[2]), F32)
        f_ref[0] = jnp.concatenate([c_ref[0], n_ref[0], pad], axis=0).astype(f_ref.dtype)


def _win_update(cache_k, cache_v, new_k, new_v):
    b, wb, w = cache_k.shape
    t = new_k.shape[1]
    full_len = wb + LANES
    cs = pl.BlockSpec((1, wb, w), lambda i: (i, 0, 0))
    ns = pl.BlockSpec((1, t, w), lambda i: (i, 0, 0))
    fs = pl.BlockSpec((1, full_len, w), lambda i: (i, 0, 0))
    return pl.pallas_call(
        _win_update_kernel, grid=(b,), in_specs=[cs, cs, ns, ns], out_specs=[cs, cs, fs, fs],
        out_shape=[jax.ShapeDtypeStruct((b, wb, w), F32)] * 2
                  + [jax.ShapeDtypeStruct((b, full_len, w), MXU_DT)] * 2,
        compiler_params=_cparams(("arbitrary",)), name="win_update",
    )(cache_k, cache_v, new_k, new_v)


def _overlap_matrix(n_cmp_pad, n_cmp, n_blocks_pad, n_blocks):
    n = jnp.arange(n_cmp_pad)[:, None]
    j = jnp.arange(n_blocks_pad)[None, :]
    c_start, s_start = n * CMP_STRIDE, j * SEL_LEN
    ov = (c_start < s_start + SEL_LEN) & (c_start + CMP_LEN > s_start) & (n < n_cmp) & (j < n_blocks)
    return ov.astype(MXU_DT)


def _finish_rows(a, w_ref, gpost_ref, x_ref, gate_ref, y_ref):
    o = jnp.dot(a.astype(MXU_DT), w_ref[...], preferred_element_type=F32)
    n = o * lax.rsqrt(jnp.mean(o * o, axis=-1, keepdims=True) + RMS_EPS) * gpost_ref[...]
    y_ref[0] = x_ref[0] + gate_ref[0] * n


def _attn_out_kernel(o_ref, z_ref, w_ref, gpost_ref, x_ref, gate_ref, y_ref):
    _finish_rows(o_ref[0] * _silu(z_ref[0]), w_ref, gpost_ref, x_ref, gate_ref, y_ref)


def _ssm_out_kernel(y_ref, z_ref, wglu_ref, w_ref, gpost_ref, x_ref, gate_ref, o_ref):
    v = jnp.dot(y_ref[0].astype(MXU_DT), wglu_ref[...], preferred_element_type=F32)
    half = v.shape[1] // 2
    glu = v[:, :half] * _sigmoid(v[:, half:])
    _finish_rows(glu * _silu(z_ref[0]), w_ref, gpost_ref, x_ref, gate_ref, o_ref)


def _mixer_out(body, name, acts, weights, g_post, x, gate):
    b, t, d = x.shape
    tm = min(t, 256)
    row = pl.BlockSpec((1, tm, d), lambda i, j: (i, j, 0))
    full = lambda a: pl.BlockSpec(a.shape, lambda i, j: (0,) * a.ndim)
    per_b = pl.BlockSpec((1, 1, d), lambda i, j: (i, 0, 0))
    return pl.pallas_call(
        body, grid=(b, t // tm),
        in_specs=[row] * len(acts) + [full(w) for w in weights] + [full(g_post), row, per_b],
        out_specs=row, out_shape=jax.ShapeDtypeStruct((b, t, d), F32),
        compiler_params=_cparams(("arbitrary", "arbitrary")), name=name,
    )(*acts, *weights, g_post, x, gate)


def _ssm_proj_kernel(x_ref, shift_ref, scale_ref, g_ref, w_ref, u_ref, z_ref):
    h = _rms_mod(x_ref[0], g_ref[...], scale_ref[0], shift_ref[0]).astype(MXU_DT)
    p = jnp.dot(h, w_ref[...], preferred_element_type=F32)
    half = p.shape[1] // 2
    u_ref[0] = p[:, :half]
    z_ref[0] = p[:, half:]


def _ssm_proj(x, shift, scale, g_pre, w):
    b, t, d = x.shape
    tm = min(t, 256)
    row = pl.BlockSpec((1, tm, d), lambda i, j: (i, j, 0))
    per_b = pl.BlockSpec((1, 1, d), lambda i, j: (i, 0, 0))
    full = lambda a: pl.BlockSpec(a.shape, lambda i, j: (0,) * a.ndim)
    out = jax.ShapeDtypeStruct((b, t, d), F32)
    return pl.pallas_call(
        _ssm_proj_kernel, grid=(b, t // tm),
        in_specs=[row, per_b, per_b, full(g_pre), full(w)], out_specs=[row, row], out_shape=[out, out],
        compiler_params=_cparams(("arbitrary", "arbitrary")), name="ssm_proj",
    )(x, shift, scale, g_pre, w)


def _s5_prep_kernel(ar_ref, ai_ref, ldt_ref, bre_ref, bim_ref, lre_ref, lim_ref, cbre_ref, cbim_ref):
    ar, ai = ar_ref[...], ai_ref[...]
    dt = jnp.exp(ldt_ref[...])
    mag = jnp.exp(dt * ar)
    lam_re, lam_im = mag * jnp.cos(dt * ai), mag * jnp.sin(dt * ai)
    den = ar * ar + ai * ai
    nr, ni = lam_re - 1.0, lam_im
    coef_re, coef_im = (nr * ar + ni * ai) / den, (ni * ar - nr * ai) / den
    lre_ref[...] = lam_re
    lim_ref[...] = lam_im
    cr, ci = coef_re[:, None, :], coef_im[:, None, :]
    b_re, b_im = bre_ref[...], bim_ref[...]
    cbre_ref[...] = cr * b_re - ci * b_im
    cbim_ref[...] = cr * b_im + ci * b_re


def _s5_prep(a_re, a_im, log_dt, b_re_t, b_im_t):
    ng, p = a_re.shape
    full = lambda a: pl.BlockSpec(a.shape, lambda: (0,) * a.ndim)
    args = (a_re, a_im, log_dt.reshape(ng, 1), b_re_t, b_im_t)
    lam = jax.ShapeDtypeStruct((ng, p), F32)
    cb = jax.ShapeDtypeStruct(b_re_t.shape, F32)
    return pl.pallas_call(
        _s5_prep_kernel, in_specs=[full(a) for a in args],
        out_specs=[full(lam), full(lam), full(cb), full(cb)], out_shape=[lam, lam, cb, cb],
        name="s5_prep",
    )(*args)


SCAN_LANES = 512


def _s5_scan_kernel(u_ref, d_ref, bre_ref, bim_ref, cre_ref, cim_ref, lre_ref, lim_ref, h0re_ref, h0im_ref,
                    y_ref, hre_ref, him_ref, sre, sim, *, n_batch, nbp, tt):
    ti = pl.program_id(1)
    n_lc = sre.shape[0]

    @pl.when(ti == 0)
    def _():
        hre_ref[...] = h0re_ref[...]
        him_ref[...] = h0im_ref[...]
        if nbp != n_batch:
            sre[...] = jnp.zeros(sre.shape, F32)
            sim[...] = jnp.zeros(sim.shape, F32)

    def drive(b, carry):
        ub = u_ref[b].astype(MXU_DT)
        for s_ref, w_ref in ((sre, bre_ref), (sim, bim_ref)):
            bb = jnp.dot(ub, w_ref[0], preferred_element_type=F32)
            for c in range(n_lc):
                s_ref[c, pl.ds(b, tt, stride=nbp), :] = bb[:, c * LANES:(c + 1) * LANES]
        return carry

    lax.fori_loop(0, n_batch, drive, 0)

    per_scan = SCAN_LANES // LANES
    for sb in range(nbp // SUBLANES):
        for lc in range(n_lc // per_scan):
            chunks = range(lc * per_scan, (lc + 1) * per_scan)
            srow = slice(sb * SUBLANES, (sb + 1) * SUBLANES)
            lane = lambda c: slice(c * LANES, (c + 1) * LANES)
            lam_re = [jnp.broadcast_to(lre_ref[:, lane(c)], (SUBLANES, LANES)) for c in chunks]
            lam_im = [jnp.broadcast_to(lim_ref[:, lane(c)], (SUBLANES, LANES)) for c in chunks]

            def step(t, carry):
                r0 = pl.multiple_of(t * nbp + sb * SUBLANES, SUBLANES)
                new = []
                for k, c in enumerate(chunks):
                    h_re, h_im = carry[k]
                    n_re = lam_re[k] * h_re - lam_im[k] * h_im + sre[c, pl.ds(r0, SUBLANES), :]
                    n_im = lam_re[k] * h_im + lam_im[k] * h_re + sim[c, pl.ds(r0, SUBLANES), :]
                    sre[c, pl.ds(r0, SUBLANES), :] = n_re
                    sim[c, pl.ds(r0, SUBLANES), :] = n_im
                    new.append((n_re, n_im))
                return tuple(new)

            init = tuple((hre_ref[srow, lane(c)], him_ref[srow, lane(c)]) for c in chunks)
            final = lax.fori_loop(0, tt, step, init, unroll=min(tt, 8))
            for k, c in enumerate(chunks):
                hre_ref[srow, lane(c)] = final[k][0]
                him_ref[srow, lane(c)] = final[k][1]

    def readout(b, carry):
        rows = lambda s_ref: jnp.concatenate(
            [s_ref[c, pl.ds(b, tt, stride=nbp), :] for c in range(n_lc)], axis=1).astype(MXU_DT)
        y = (jnp.dot(rows(sre), cre_ref[0], preferred_element_type=F32)
             - jnp.dot(rows(sim), cim_ref[0], preferred_element_type=F32))
        y_ref[b] = y + d_ref[...] * u_ref[b]
        return carry

    lax.fori_loop(0, n_batch, readout, 0)


def _s5_scan(u, d_skip, b_re_bd, b_im_bd, c_re_bd, c_im_bd, lam_re, lam_im, h0_re, h0_im, nbp):
    b, t, w = u.shape
    q, wq, sq = b_re_bd.shape
    ns = q * sq
    tt = min(t, 256)
    ublk = pl.BlockSpec((b, tt, wq), lambda k, i: (0, i, k))
    dblk = pl.BlockSpec((1, wq), lambda k, i: (0, k))
    bblk = pl.BlockSpec((1, wq, sq), lambda k, i: (k, 0, 0))
    cblk = pl.BlockSpec((1, sq, wq), lambda k, i: (k, 0, 0))
    lblk = pl.BlockSpec((1, sq), lambda k, i: (0, k))
    hblk = pl.BlockSpec((nbp, sq), lambda k, i: (0, k))
    st = jax.ShapeDtypeStruct((nbp, ns), F32)
    return pl.pallas_call(
        functools.partial(_s5_scan_kernel, n_batch=b, nbp=nbp, tt=tt),
        grid=(q, t // tt),
        in_specs=[ublk, dblk, bblk, bblk, cblk, cblk, lblk, lblk, hblk, hblk],
        out_specs=[ublk, hblk, hblk],
        out_shape=[jax.ShapeDtypeStruct((b, t, w), F32), st, st],
        scratch_shapes=[pltpu.VMEM((sq // LANES, tt * nbp, LANES), F32)] * 2,
        compiler_params=_cparams(("arbitrary", "arbitrary")), name="s5_scan",
    )(u, d_skip, b_re_bd, b_im_bd, c_re_bd, c_im_bd, lam_re, lam_im, h0_re, h0_im)


def _s5_block_diag(cb_re, cb_im, c_re, c_im):
    ng, c, p = cb_re.shape
    gq = ng // S5_QUARTERS
    eye = jnp.eye(gq, dtype=F32)

    def drive(w):
        w4 = w.reshape(S5_QUARTERS, gq, c, p)
        return jnp.einsum('qgcp,gk->qgckp', w4, eye).reshape(S5_QUARTERS, gq * c, gq * p).astype(MXU_DT)

    def readout(w):
        w4 = w.reshape(S5_QUARTERS, gq, c, p)
        return jnp.einsum('qgcp,gk->qgpkc', w4, eye).reshape(S5_QUARTERS, gq * p, gq * c).astype(MXU_DT)

    return drive(cb_re), drive(cb_im), readout(c_re), readout(c_im)


def _nsa_layer(xp, xs, modp, mods, caches, page_table, g_pre, g_post, w_in, pe, wk1, wk2, wv1, wv2, w_out):
    cache_kc, cache_vc, cache_ks, cache_vs, cache_kw, cache_vw = caches
    b, t, d = xp.shape
    sb, st, _ = xs.shape
    n_pages = page_table.shape[1]
    past = n_pages * PAGE_SIZE

    aw = N_HEADS * HEAD_DIM
    cuts = [aw, aw + 6 * KV_WIDTH, aw + 6 * KV_WIDTH + 3 * N_HEADS]
    wq = w_in[:, :cuts[0]].astype(MXU_DT)
    wkv = w_in[:, cuts[0]:cuts[1]].astype(MXU_DT)
    wg = jnp.pad(w_in[:, cuts[1]:cuts[2]], ((0, 0), (0, LANES - 3 * N_HEADS))).astype(MXU_DT)
    wz = w_in[:, cuts[2]:].astype(MXU_DT)
    g_pre2, g_post2 = g_pre.reshape(1, d), g_post.reshape(1, d)

    proj_p = _attn_proj(xp, modp[0], modp[1], g_pre2, _rope_tables(jnp.arange(t)), wq, wkv, wg, wz)
    proj_s = _attn_proj(xs, mods[0], mods[1], g_pre2, _rope_tables(past + jnp.arange(st)), wq, wkv, wg, wz)
    (q_p, kc_p, vc_p, ks_p, vs_p, kw_p, vw_p, ksb_p, vsb_p, kwb_p, vwb_p, gl_p, z_p) = proj_p
    (q_s, kc_s, vc_s, ks_s, vs_s, kw_s, vw_s, ksb_s, vsb_s, _, _, gl_s, z_s) = proj_s

    wk_big, w1k_t, w2k_bd = _cmp_weights(wk1, wk2)
    wv_big, w1v_t, w2v_bd = _cmp_weights(wv1, wv2)
    pe_col = pe.reshape(CMP_LEN * HEAD_DIM, 1)
    chunk_w = CMP_STRIDE * KV_WIDTH

    n_ch = t // CMP_STRIDE
    pk, pv = _cmp_part(kc_p.reshape(b * n_ch, chunk_w), vc_p.reshape(b * n_ch, chunk_w), wk_big, wv_big)
    kcc_p, vcc_p = _cmp_finish(pk.reshape(b, n_ch, -1), pv.reshape(b, n_ch, -1), pe_col,
                               w1k_t, w1v_t, w2k_bd, w2v_bd)
    ns_p = -(-t // SEL_LEN)
    nbp_p = -(-ns_p // LANES) * LANES
    ov_p = _overlap_matrix(n_ch, n_ch - CMP_RATIO + 1, nbp_p, ns_p)
    o_p = _attn_prompt(q_p, gl_p, kcc_p, vcc_p, ov_p, ksb_p, vsb_p, kwb_p, vwb_p)
    y_p = _mixer_out(_attn_out_kernel, "attn_out", (o_p, z_p), (w_out.astype(MXU_DT),), g_post2, xp, modp[2])

    cpp = PAGE_SIZE // CMP_STRIDE
    n_phys = cache_kc.shape[0]
    pk, pv = _cmp_part_paged(page_table, cache_kc.reshape(n_phys, cpp, chunk_w),
                             cache_vc.reshape(n_phys, cpp, chunk_w), wk_big, wv_big)
    kcc_s, vcc_s = _cmp_finish(pk, pv, pe_col, w1k_t, w1v_t, w2k_bd, w2v_bd)
    n_ch_s = (past + st) // CMP_STRIDE
    ns_s = -(-(past + st) // SEL_LEN)
    nbp_s = -(-ns_s // LANES) * LANES
    ov_s = _overlap_matrix(n_pages * cpp, n_ch_s - CMP_RATIO + 1, nbp_s, ns_s)
    wb = cache_kw.shape[1]
    win_k, win_v, kw_full, vw_full = _win_update(cache_kw.reshape(sb, wb, KV_WIDTH),
                                                 cache_vw.reshape(sb, wb, KV_WIDTH), kw_s, vw_s)
    pad_new = lambda a: jnp.pad(a, ((0, 0), (0, KEY_TILE - st), (0, 0)))
    o_s = _attn_sample(page_table, q_s, gl_s, kcc_s, vcc_s, ov_s, kw_full, vw_full,
                       pad_new(ksb_s), pad_new(vsb_s),
                       cache_ks.reshape(n_phys, PAGE_SIZE, KV_WIDTH), cache_vs.reshape(n_phys, PAGE_SIZE, KV_WIDTH))
    y_s = _mixer_out(_attn_out_kernel, "attn_out_s", (o_s, z_s), (w_out.astype(MXU_DT),), g_post2, xs, mods[2])

    heads = lambda a: a.reshape(a.shape[0], a.shape[1], N_KV, HEAD_DIM)
    wl = min(WINDOW, t)
    rows_p = (kc_p, vc_p, ks_p, vs_p, kw_p[:, t - wl:], vw_p[:, t - wl:])
    rows_s = (kc_s, vc_s, ks_s, vs_s, win_k, win_v)
    return y_p, y_s, [heads(a) for a in rows_p], [heads(a) for a in rows_s]


def _s5_layer(xp, xs, modp, mods, state_re, state_im, g_pre, g_post, w_in, a_re, a_im, log_dt,
              b_re, b_im, c_re, c_im, d_skip, w_glu, w_out):
    b, t, d = xp.shape
    sb, st, _ = xs.shape
    ng, p, c = b_re.shape
    g_pre2, g_post2 = g_pre.reshape(1, d), g_post.reshape(1, d)
    w_in16, w_glu16, w_out16 = w_in.astype(MXU_DT), w_glu.astype(MXU_DT), w_out.astype(MXU_DT)
    lam_re, lam_im, cb_re, cb_im = _s5_prep(a_re, a_im, log_dt, b_re.transpose(0, 2, 1), b_im.transpose(0, 2, 1))
    bd = _s5_block_diag(cb_re, cb_im, c_re, c_im)
    lam = (lam_re.reshape(1, ng * p), lam_im.reshape(1, ng * p))
    d2 = d_skip.reshape(1, d)

    outs = []
    for x, mod, h0 in ((xp, modp, None), (xs, mods, (state_re, state_im))):
        nb = x.shape[0]
        nbp = -(-nb // SUBLANES) * SUBLANES
        u, z = _ssm_proj(x, mod[0], mod[1], g_pre2, w_in16)
        if h0 is None:
            h0_re = h0_im = jnp.zeros((nbp, ng * p), F32)
        else:
            h0_re, h0_im = (jnp.pad(h.reshape(nb, ng * p), ((0, nbp - nb), (0, 0))) for h in h0)
        y, h_re, h_im = _s5_scan(u, d2, *bd, *lam, h0_re, h0_im, nbp)
        out = _mixer_out(_ssm_out_kernel, "ssm_out", (y, z), (w_glu16, w_out16), g_post2, x, mod[2])
        outs.append((out, h_re[:nb].reshape(nb, ng, p), h_im[:nb].reshape(nb, ng, p)))
    return outs


def kernel(x_prompt, x_sample, cache_k_cmp, cache_v_cmp, cache_k_sel, cache_v_sel, cache_k_win, cache_v_win, state_s5_re, state_s5_im, page_table, c_prompt, c_sample, norm_pre, norm_post, w_ada, b_ada, w_in_attn, pe_cmp, w_cmp_k1, w_cmp_k2, w_cmp_v1, w_cmp_v2, w_out_attn, w_in_ssm, s5_a_re, s5_a_im, s5_log_dt, s5_b_re, s5_b_im, s5_c_re, s5_c_im, s5_d, w_glu, w_out_ssm):
    b, t, d = x_prompt.shape
    sb = x_sample.shape[0]
    depth = w_ada.shape[0]
    c_all = jnp.concatenate([c_prompt, c_sample], axis=0)
    n_c = -(-c_all.shape[0] // SUBLANES) * SUBLANES
    mod = _ada(jnp.pad(c_all, ((0, n_c - c_all.shape[0]), (0, 0))), w_ada, b_ada)

    yp, ys = x_prompt, x_sample
    attn_p, attn_s, ssm_p, ssm_s = [], [], [], []
    for i in range(depth):
        parts = [mod[i, :, k * d:(k + 1) * d] for k in range(3)]
        modp = [a[:b].reshape(b, 1, d) for a in parts]
        mods = [a[b:b + sb].reshape(sb, 1, d) for a in parts]
        li = i // 2
        if i % 2 == 0:
            caches = (cache_k_cmp[li], cache_v_cmp[li], cache_k_sel[li], cache_v_sel[li],
                      cache_k_win[li], cache_v_win[li])
            yp, ys, rows_p, rows_s = _nsa_layer(
                yp, ys, modp, mods, caches, page_table, norm_pre[i], norm_post[i], w_in_attn[li], pe_cmp[li],
                w_cmp_k1[li], w_cmp_k2[li], w_cmp_v1[li], w_cmp_v2[li], w_out_attn[li])
            attn_p.append(rows_p)
            attn_s.append(rows_s)
        else:
            (yp, pr, pi), (ys, sr, si) = _s5_layer(
                yp, ys, modp, mods, state_s5_re[li], state_s5_im[li], norm_pre[i], norm_post[i], w_in_ssm[li],
                s5_a_re[li], s5_a_im[li], s5_log_dt[li], s5_b_re[li], s5_b_im[li], s5_c_re[li], s5_c_im[li],
                s5_d[li], w_glu[li], w_out_ssm[li])
            ssm_p.append((pr, pi))
            ssm_s.append((sr, si))

    outs = [yp, ys]
    for n in range(6):
        outs.append(jnp.stack([rows[n] for rows in attn_p]))
        outs.append(jnp.stack([rows[n] for rows in attn_s]))
    for n in range(2):
        outs.append(jnp.stack([st[n] for st in ssm_p]))
        outs.append(jnp.stack([st[n] for st in ssm_s]))
    return tuple(outs)
```

```python
import functools
import math

import jax
import jax.numpy as jnp
from jax import lax
from jax.experimental import pallas as pl
from jax.experimental.pallas import tpu as pltpu

N_HEADS = 16
HEAD_DIM = 64
N_KV = 4
GROUP_R = N_HEADS // N_KV
ROT_HALF = HEAD_DIM // 8
ROPE_THETA = 500000.0
CMP_LEN = 32
CMP_STRIDE = 16
CMP_RATIO = CMP_LEN // CMP_STRIDE
CMP_HIDDEN = 64
SEL_LEN = 64
SEL_SHIFT = SEL_LEN.bit_length() - 1
N_SEL = 16
WINDOW = 512
PAGE_SIZE = 128
S5_GROUP = 16
S5_STATE = 64
NEG = -1e30
SEL_FORCE = 1e4
RMS_EPS = 1e-6
KV_WIDTH = N_KV * HEAD_DIM

LANES = 128
SUBLANES = 8
VMEM_LIMIT_BYTES = 56 * 1024 * 1024

MXU_DT = jnp.bfloat16
F32 = jnp.float32

KEY_TILE = 256
BLOCKS_PER_TILE = KEY_TILE // SEL_LEN
SAMPLE_KEY_TILE = 1024
PAGES_PER_STEP = 32
S5_QUARTERS = 4


def _cparams(sem):
    return pltpu.CompilerParams(dimension_semantics=sem, vmem_limit_bytes=VMEM_LIMIT_BYTES)


def _dot(a, b):
    return jnp.dot(a.astype(MXU_DT), b.astype(MXU_DT), preferred_element_type=F32)


def _dot_nt(a, b):
    return lax.dot_general(a.astype(MXU_DT), b.astype(MXU_DT), (((1,), (1,)), ((), ())),
                           preferred_element_type=F32)


def _split3(x):
    hi = x.astype(MXU_DT)
    r1 = x - hi.astype(F32)
    mid = r1.astype(MXU_DT)
    lo = (r1 - mid.astype(F32)).astype(MXU_DT)
    return hi, mid, lo


def _dot_exact_rhs(x, m):
    hi, mid, lo = _split3(x)
    return (jnp.dot(hi, m, preferred_element_type=F32) + jnp.dot(mid, m, preferred_element_type=F32)
            + jnp.dot(lo, m, preferred_element_type=F32))


def _dot_f32(a, b):
    ah, am, _ = _split3(a)
    bh, bm, _ = _split3(b)
    d = lambda p, q: jnp.dot(p, q, preferred_element_type=F32)
    return d(ah, bh) + (d(ah, bm) + d(am, bh)) + d(am, bm)


def _sigmoid(x):
    return 1.0 / (1.0 + jnp.exp(-x))


def _silu(x):
    return x * _sigmoid(x)


def _ada_kernel(c_ref, w_ref, b_ref, o_ref):
    o_ref[...] = _dot_f32(c_ref[...], w_ref[...]) + b_ref[...]


def _ada(c_all, w_ada, b_ada):
    depth, d, d3 = w_ada.shape
    n = c_all.shape[0]
    return pl.pallas_call(
        _ada_kernel,
        grid=(depth, d3 // d),
        in_specs=[pl.BlockSpec((n, d), lambda i, j: (0, 0)),
                  pl.BlockSpec((None, d, d), lambda i, j: (i, 0, j)),
                  pl.BlockSpec((None, 1, d), lambda i, j: (i, 0, j))],
        out_specs=pl.BlockSpec((None, n, d), lambda i, j: (i, 0, j)),
        out_shape=jax.ShapeDtypeStruct((depth, n, d3), F32),
        compiler_params=_cparams(("arbitrary", "arbitrary")),
        name="ada_mod",
    )(c_all, w_ada, b_ada.reshape(depth, 1, d3))


def _rms_mod(x, g, scale, shift):
    y = x * lax.rsqrt(jnp.mean(x * x, axis=-1, keepdims=True) + RMS_EPS)
    return (y * g) * (1.0 + scale) + shift


def _rope_slab(x, c, s1, s2):
    return x * c + pltpu.roll(x, ROT_HALF, 1) * s1 + pltpu.roll(x, LANES - ROT_HALF, 1) * s2


def _attn_proj_kernel(x_ref, shift_ref, scale_ref, g_ref, c_ref, s1_ref, s2_ref,
                      wq_ref, wkv_ref, wg_ref, wz_ref,
                      q_ref, kc_ref, vc_ref, ks_ref, vs_ref, kw_ref, vw_ref,
                      ksb_ref, vsb_ref, kwb_ref, vwb_ref, gl_ref, z_ref):
    h = _rms_mod(x_ref[0], g_ref[...], scale_ref[0], shift_ref[0]).astype(MXU_DT)
    c, s1, s2 = c_ref[...], s1_ref[...], s2_ref[...]
    q = jnp.dot(h, wq_ref[...], preferred_element_type=F32)
    qscale = HEAD_DIM ** -0.5
    for j in range(q.shape[1] // LANES):
        sl = slice(j * LANES, (j + 1) * LANES)
        q_ref[0, :, sl] = (_rope_slab(q[:, sl], c, s1, s2) * qscale).astype(q_ref.dtype)
    kv = jnp.dot(h, wkv_ref[...], preferred_element_type=F32)
    outs = ((kc_ref, None, False), (vc_ref, None, False), (ks_ref, ksb_ref, True),
            (vs_ref, vsb_ref, False), (kw_ref, kwb_ref, True), (vw_ref, vwb_ref, False))
    for n, (o_ref, ob_ref, roped) in enumerate(outs):
        for j in range(KV_WIDTH // LANES):
            lo = n * KV_WIDTH + j * LANES
            v = kv[:, lo:lo + LANES]
            if roped:
                v = _rope_slab(v, c, s1, s2)
            o_ref[0, :, j * LANES:(j + 1) * LANES] = v
            if ob_ref is not None:
                ob_ref[0, :, j * LANES:(j + 1) * LANES] = v.astype(ob_ref.dtype)
    gl_ref[0] = jnp.dot(h, wg_ref[...], preferred_element_type=F32)
    z_ref[0] = jnp.dot(h, wz_ref[...], preferred_element_type=F32)


def _attn_proj(x, shift, scale, g_pre, tables, wq, wkv, wg, wz):
    b, t, d = x.shape
    tm = min(t, 256)
    cos_t, s1_t, s2_t = tables
    row = lambda i, j: (i, j, 0)
    per_b = pl.BlockSpec((1, 1, d), lambda i, j: (i, 0, 0))
    full = lambda a: pl.BlockSpec(a.shape, lambda i, j: (0,) * a.ndim)
    tab = pl.BlockSpec((tm, LANES), lambda i, j: (j, 0))
    kv32 = jax.ShapeDtypeStruct((b, t, KV_WIDTH), F32)
    kv16 = jax.ShapeDtypeStruct((b, t, KV_WIDTH), MXU_DT)
    kvspec = pl.BlockSpec((1, tm, KV_WIDTH), row)
    return pl.pallas_call(
        _attn_proj_kernel,
        grid=(b, t // tm),
        in_specs=[pl.BlockSpec((1, tm, d), row), per_b, per_b, full(g_pre), tab, tab, tab,
                  full(wq), full(wkv), full(wg), full(wz)],
        out_specs=[pl.BlockSpec((1, tm, d), row)] + [kvspec] * 10
                  + [pl.BlockSpec((1, tm, LANES), row), pl.BlockSpec((1, tm, d), row)],
        out_shape=[jax.ShapeDtypeStruct((b, t, d), MXU_DT)] + [kv32] * 6 + [kv16] * 4
                  + [jax.ShapeDtypeStruct((b, t, LANES), F32), jax.ShapeDtypeStruct((b, t, d), F32)],
        compiler_params=_cparams(("arbitrary", "arbitrary")),
        name="attn_proj",
    )(x, shift, scale, g_pre, cos_t, s1_t, s2_t, wq, wkv, wg, wz)


def _rope_tables(pos):
    inv = ROPE_THETA ** (-jnp.arange(ROT_HALF, dtype=F32) / ROT_HALF)
    ang = pos.astype(F32)[:, None] * inv[None]
    cos, sin = jnp.cos(ang), jnp.sin(ang)
    n = pos.shape[0]
    rest = HEAD_DIM - 2 * ROT_HALF
    zeros = lambda w: jnp.zeros((n, w), F32)
    c64 = jnp.concatenate([cos, cos, jnp.ones((n, rest), F32)], axis=1)
    s1 = jnp.concatenate([zeros(ROT_HALF), sin, zeros(rest)], axis=1)
    s2 = jnp.concatenate([-sin, zeros(ROT_HALF + rest)], axis=1)
    rep = LANES // HEAD_DIM
    return tuple(jnp.tile(a, (1, rep)) for a in (c64, s1, s2))


def _cmp_part_kernel(xk_ref, xv_ref, wk_ref, wv_ref, pk_ref, pv_ref):
    pk_ref[...] = _dot(xk_ref[...], wk_ref[...])
    pv_ref[...] = _dot(xv_ref[...], wv_ref[...])


def _cmp_part(xk, xv, wk, wv):
    rows, width = xk.shape
    tm = min(rows, 256)
    n = wk.shape[1]
    xs = pl.BlockSpec((tm, width), lambda i: (i, 0))
    ws = pl.BlockSpec(wk.shape, lambda i: (0, 0))
    os_ = pl.BlockSpec((tm, n), lambda i: (i, 0))
    return pl.pallas_call(
        _cmp_part_kernel, grid=(rows // tm,), in_specs=[xs, xs, ws, ws], out_specs=[os_, os_],
        out_shape=[jax.ShapeDtypeStruct((rows, n), F32)] * 2,
        compiler_params=_cparams(("arbitrary",)), name="cmp_part",
    )(xk, xv, wk, wv)


def _page_copy(pool_ref, pt_ref, b, page, buf_ref, slot, sem):
    return pltpu.make_async_copy(pool_ref.at[pt_ref[b, page]], buf_ref.at[slot], sem)


def _gather_pages(pt_ref, b, first_page, pools, bufs, sems):
    def start(p, _):
        for pool, buf, k in zip(pools, bufs, range(len(pools))):
            _page_copy(pool, pt_ref, b, first_page + p, buf, p, sems.at[k]).start()
        return 0

    def wait(p, _):
        for pool, buf, k in zip(pools, bufs, range(len(pools))):
            _page_copy(pool, pt_ref, b, first_page + p, buf, p, sems.at[k]).wait()
        return 0

    lax.fori_loop(0, PAGES_PER_STEP, start, 0)
    lax.fori_loop(0, PAGES_PER_STEP, wait, 0)


def _cmp_part_paged_kernel(pt_ref, poolk_ref, poolv_ref, perm_ref, wk_ref, wv_ref, pk_ref, pv_ref,
                           bufk, bufv, xk, xv, sems):
    b, c = pl.program_id(0), pl.program_id(1)
    _gather_pages(pt_ref, b, c * PAGES_PER_STEP, (poolk_ref, poolv_ref), (bufk, bufv), sems)
    cpp = PAGE_SIZE // CMP_STRIDE
    perm = perm_ref[...]

    def page_rows(p, carry):
        r0 = pl.multiple_of(p * cpp, cpp)
        for buf, x in ((bufk, xk), (bufv, xv)):
            y = _dot_nt(perm, buf[p])
            for s in range(CMP_STRIDE):
                x[pl.ds(r0, cpp), s * KV_WIDTH:(s + 1) * KV_WIDTH] = y[s * cpp:(s + 1) * cpp]
        return carry

    lax.fori_loop(0, PAGES_PER_STEP, page_rows, 0, unroll=4)
    pk_ref[0] = _dot(xk[...], wk_ref[...])
    pv_ref[0] = _dot(xv[...], wv_ref[...])


def _cmp_part_paged(page_table, poolk, poolv, wk, wv):
    nb, n_pages = page_table.shape
    cpp = PAGE_SIZE // CMP_STRIDE
    assert cpp == SUBLANES
    n = wk.shape[1]
    rows = PAGES_PER_STEP * cpp
    slot = jnp.arange(PAGE_SIZE)
    perm = (slot[None, :] == (slot[:, None] % cpp) * CMP_STRIDE + slot[:, None] // cpp).astype(MXU_DT)
    any_spec = pl.BlockSpec(memory_space=pl.ANY)
    full = lambda a: pl.BlockSpec(a.shape, lambda b, c, pt: (0, 0))
    os_ = pl.BlockSpec((1, rows, n), lambda b, c, pt: (b, c, 0))
    page_buf = pltpu.VMEM((PAGES_PER_STEP, KV_WIDTH, PAGE_SIZE), F32)
    chunk_rows = pltpu.VMEM((rows, CMP_STRIDE * KV_WIDTH), F32)
    grid_spec = pltpu.PrefetchScalarGridSpec(
        num_scalar_prefetch=1, grid=(nb, n_pages // PAGES_PER_STEP),
        in_specs=[any_spec, any_spec, full(perm), full(wk), full(wv)], out_specs=[os_, os_],
        scratch_shapes=[page_buf, page_buf, chunk_rows, chunk_rows, pltpu.SemaphoreType.DMA((2,))])
    return pl.pallas_call(
        _cmp_part_paged_kernel, grid_spec=grid_spec,
        out_shape=[jax.ShapeDtypeStruct((nb, n_pages * cpp, n), F32)] * 2,
        compiler_params=_cparams(("arbitrary", "arbitrary")), name="cmp_part_paged",
    )(page_table, poolk, poolv, perm, wk, wv)


def _cmp_finish_kernel(pk_ref, pv_ref, pe_ref, w1k_ref, w1v_ref, w2k_ref, w2v_ref, kc_ref, vc_ref):
    pe = pe_ref[...]
    for p_ref, w1_ref, w2_ref, o_ref in ((pk_ref, w1k_ref, w2k_ref, kc_ref),
                                         (pv_ref, w1v_ref, w2v_ref, vc_ref)):
        p = p_ref[0]
        n_ch = p.shape[0]
        p0, p1 = p[:, :KV_WIDTH], p[:, KV_WIDTH:]
        row = lax.broadcasted_iota(jnp.int32, p1.shape, 0)
        p1_next = jnp.where(row == n_ch - 1, 0.0, pltpu.roll(p1, n_ch - 1, 0))
        bias = jnp.sum(pe * w1_ref[...], axis=0, keepdims=True)
        o_ref[0] = _dot(_silu(p0 + p1_next + bias), w2_ref[...]).astype(o_ref.dtype)


def _cmp_finish(pk, pv, pe_col, w1k_t, w1v_t, w2k_bd, w2v_bd):
    nb, n_ch, n = pk.shape
    ps = pl.BlockSpec((1, n_ch, n), lambda b: (b, 0, 0))
    full = lambda a: pl.BlockSpec(a.shape, lambda b: (0,) * a.ndim)
    os_ = pl.BlockSpec((1, n_ch, KV_WIDTH), lambda b: (b, 0, 0))
    return pl.pallas_call(
        _cmp_finish_kernel, grid=(nb,),
        in_specs=[ps, ps, full(pe_col), full(w1k_t), full(w1v_t), full(w2k_bd), full(w2v_bd)],
        out_specs=[os_, os_],
        out_shape=[jax.ShapeDtypeStruct((nb, n_ch, KV_WIDTH), MXU_DT)] * 2,
        compiler_params=_cparams(("arbitrary",)), name="cmp_finish",
    )(pk, pv, pe_col, w1k_t, w1v_t, w2k_bd, w2v_bd)


def _cmp_weights(w1, w2):
    eye = jnp.eye(N_KV, dtype=F32)
    w1r = w1.reshape(CMP_RATIO, CMP_STRIDE, HEAD_DIM, CMP_HIDDEN)
    big = jnp.einsum('rsdh,gk->sgdrkh', w1r, eye).reshape(CMP_STRIDE * KV_WIDTH, CMP_RATIO * KV_WIDTH)
    w1_t = jnp.tile(w1, (1, N_KV))
    w2_bd = jnp.einsum('hd,gk->ghkd', w2, eye).reshape(N_KV * CMP_HIDDEN, KV_WIDTH)
    return big.astype(MXU_DT), w1_t, w2_bd.astype(MXU_DT)


def _lane_half_mask(shape, half):
    lane = lax.broadcasted_iota(jnp.int32, shape, len(shape) - 1)
    return (lane >= HEAD_DIM) if half else (lane < HEAD_DIM)


def _build_q4(q_ref, q4_ref, tq):
    del tq
    for g in range(N_KV):
        hg = g % 2
        parts = []
        for r in range(GROUP_R):
            h = g * GROUP_R + r
            slab = q_ref[0, :, (h // 2) * LANES:(h // 2 + 1) * LANES].astype(F32)
            if h % 2 != hg:
                slab = pltpu.roll(slab, HEAD_DIM, 1)
            parts.append(jnp.where(_lane_half_mask(slab.shape, hg), slab, 0.0))
        q4_ref[g] = jnp.concatenate(parts, axis=0).astype(q4_ref.dtype)


def _row_qpos(rows, cols, tq, q0):
    t = lax.broadcasted_iota(jnp.int32, (rows, cols), 0) & (tq - 1)
    return q0 + t


def _add_head_bias(s, bias, tq):
    rows, width = s.shape
    return (s.reshape(GROUP_R, tq, width) + bias[None]).reshape(rows, width)


def _softmax_attend(q4, k, v, bias, tq):
    t = _add_head_bias(_dot_nt(q4, k), bias, tq)
    m = jnp.max(t, axis=1, keepdims=True)
    e = jnp.exp(t - m)
    l = jnp.sum(e, axis=1, keepdims=True)
    inv = jnp.where(m > 0.5 * NEG, 1.0 / l, 0.0)
    return _dot(e, v) * inv, e, inv


def _top_blocks(imp, qpos, n_blocks):
    j = lax.broadcasted_iota(jnp.int32, imp.shape, 1)
    cur = qpos >> SEL_SHIFT
    forced = (j == 0) | (j == cur) | (j == cur - 1)
    valid = j * SEL_LEN <= qpos
    v = jnp.where(valid, imp + jnp.where(forced, SEL_FORCE, 0.0), -SEL_FORCE)
    v = jnp.where(j < n_blocks, v, -jnp.inf)
    sel = jnp.zeros(imp.shape, F32)
    jf = j.astype(F32)
    for _ in range(N_SEL):
        m = jnp.max(v, axis=1, keepdims=True)
        first = jnp.min(jnp.where(v == m, jf, float(imp.shape[1])), axis=1, keepdims=True)
        hit = jf == first
        sel = jnp.where(hit, 1.0, sel)
        v = jnp.where(hit, -jnp.inf, v)
    return sel


def _compressed_and_select(q4_ref, kc_ref, vc_ref, ov_ref, oc_ref, selneg_ref, tq, q0, n_blocks):
    ncp = kc_ref.shape[1]
    n = lax.broadcasted_iota(jnp.int32, (tq, ncp), 1)
    bias_c = jnp.where(n * CMP_STRIDE + (CMP_LEN - 1) <= _row_qpos(tq, ncp, tq, q0), 0.0, NEG)
    psums = []
    for g in range(N_KV):
        sl = slice((g // 2) * LANES, (g // 2 + 1) * LANES)
        o, e, inv = _softmax_attend(q4_ref[g], kc_ref[0, :, sl], vc_ref[0, :, sl], bias_c, tq)
        oc_ref[g] = o
        p = e * inv
        ps = p[0:tq]
        for r in range(1, GROUP_R):
            ps = ps + p[r * tq:(r + 1) * tq]
        psums.append(ps)
    imp = _dot_exact_rhs(jnp.concatenate(psums, axis=0), ov_ref[...])
    qpos = _row_qpos(N_KV * tq, imp.shape[1], tq, q0)
    sel = _top_blocks(imp, qpos, n_blocks)
    selneg_ref[...] = jnp.where(sel > 0.5, 0.0, NEG).astype(selneg_ref.dtype)


def _selected_tile_update(q4_ref, k_of, v1_of, kv_feature_major, selneg, key_tile, first_block, key0,
                          tq, q0, m_ref, acc_ref):
    nb = selneg.shape[1]
    jj = lax.broadcasted_iota(jnp.int32, (nb, key_tile), 0)
    kk = lax.broadcasted_iota(jnp.int32, (nb, key_tile), 1)
    expand = jnp.where(jj == first_block + (kk >> SEL_SHIFT), 1.0, 0.0).astype(MXU_DT)
    bias_sel = jnp.dot(selneg, expand, preferred_element_type=F32)
    kpos = key0 + lax.broadcasted_iota(jnp.int32, (tq, key_tile), 1)
    causal = jnp.where(kpos <= _row_qpos(tq, key_tile, tq, q0), 0.0, NEG)
    for g in range(N_KV):
        bias = bias_sel[g * tq:(g + 1) * tq] + causal
        q4 = q4_ref[g]
        s = jnp.dot(q4, k_of(g), preferred_element_type=F32) if kv_feature_major else _dot_nt(q4, k_of(g))
        t = _add_head_bias(s, bias, tq)
        m_old = m_ref[g]
        m_new = jnp.maximum(m_old, jnp.max(t, axis=1, keepdims=True))
        alpha = jnp.exp(m_old - m_new)
        e = jnp.concatenate([jnp.exp(t[:, c * LANES:(c + 1) * LANES] - m_new)
                             for c in range(key_tile // LANES)], axis=1).astype(MXU_DT)
        pv = _dot_nt(e, v1_of(g)) if kv_feature_major else jnp.dot(e, v1_of(g), preferred_element_type=F32)
        acc_ref[g] = alpha * acc_ref[g] + pv
        m_ref[g] = m_new


def _init_softmax_state(m_ref, acc_ref):
    m_ref[...] = jnp.full(m_ref.shape, NEG, F32)
    acc_ref[...] = jnp.zeros(acc_ref.shape, F32)


def _combine_and_store(gl_ref, oc_ref, ow_ref, selected_of, o_ref, tq):
    gates = _sigmoid(gl_ref[0])
    for g in range(N_KV):
        hg = g % 2
        o_s = selected_of(g)
        o_c, o_w = oc_ref[g], ow_ref[g]
        heads = []
        for r in range(GROUP_R):
            h = g * GROUP_R + r
            rs = slice(r * tq, (r + 1) * tq)
            gc = [gates[:, 3 * h + j:3 * h + j + 1] for j in range(3)]
            x = gc[0] * o_c[rs] + gc[1] * o_s[rs] + gc[2] * o_w[rs]
            if h % 2 != hg:
                x = pltpu.roll(x, HEAD_DIM, 1)
            heads.append(x)
        for i in range(GROUP_R // 2):
            slab = jnp.where(_lane_half_mask(heads[0].shape, 0), heads[2 * i], heads[2 * i + 1])
            lo = (g * GROUP_R // 2 + i) * LANES
            o_ref[0, :, lo:lo + LANES] = slab


def _window_attend(q4_ref, kw, vw, bias, ow_ref, tq):
    for g in range(N_KV):
        sl = slice((g // 2) * LANES, (g // 2 + 1) * LANES)
        ow_ref[g] = _softmax_attend(q4_ref[g], kw[:, sl], vw[:, sl], bias, tq)[0]


def _attn_scratch(rows, tq, nbp):
    per_group = lambda dt: pltpu.VMEM((N_KV, rows, LANES), dt)
    return [per_group(MXU_DT), per_group(F32), per_group(F32), pltpu.VMEM((N_KV * tq, nbp), MXU_DT),
            per_group(F32), per_group(F32)]


def _ones_in_other_half(v, hg, axis):
    idx = lax.broadcasted_iota(jnp.int32, v.shape, axis)
    own = (idx >= HEAD_DIM) if hg else (idx < HEAD_DIM)
    return jnp.where(own, v, jnp.ones_like(v))


def _attn_prompt_kernel(*refs, tq, n_win):
    (q_ref, gl_ref, kc_ref, vc_ref, ov_ref, ks_ref, vs_ref) = refs[:7]
    kw_refs = refs[7:7 + n_win]
    vw_refs = refs[7 + n_win:7 + 2 * n_win]
    o_ref = refs[7 + 2 * n_win]
    q4_ref, oc_ref, ow_ref, selneg_ref, m_ref, acc_ref = refs[8 + 2 * n_win:]
    i = pl.program_id(1)
    q0 = i * tq
    n_blocks = ks_ref.shape[1] // SEL_LEN

    _build_q4(q_ref, q4_ref, tq)
    _compressed_and_select(q4_ref, kc_ref, vc_ref, ov_ref, oc_ref, selneg_ref, tq, q0, n_blocks)

    kw = jnp.concatenate([r[0] for r in kw_refs], axis=0)
    vw = jnp.concatenate([r[0] for r in vw_refs], axis=0)
    lw = kw.shape[0]
    kpos = (i - (n_win - 1)) * tq + lax.broadcasted_iota(jnp.int32, (tq, lw), 1)
    qpos = _row_qpos(tq, lw, tq, q0)
    in_window = (kpos <= qpos) & (kpos >= qpos - WINDOW) & (kpos >= 0)
    _window_attend(q4_ref, kw, vw, jnp.where(in_window, 0.0, NEG), ow_ref, tq)

    _init_softmax_state(m_ref, acc_ref)
    selneg = selneg_ref[...]

    def body(kt, carry):
        k0 = pl.multiple_of(kt * KEY_TILE, KEY_TILE)
        slab = lambda g: slice((g // 2) * LANES, (g // 2 + 1) * LANES)
        k_of = lambda g: ks_ref[0, pl.ds(k0, KEY_TILE), slab(g)]
        v1_of = lambda g: _ones_in_other_half(vs_ref[0, pl.ds(k0, KEY_TILE), slab(g)], g % 2, 1)
        _selected_tile_update(q4_ref, k_of, v1_of, False, selneg, KEY_TILE, kt * BLOCKS_PER_TILE, k0,
                              tq, q0, m_ref, acc_ref)
        return carry

    lax.fori_loop(0, (q0 + tq + KEY_TILE - 1) // KEY_TILE, body, 0)

    def selected_of(g):
        acc = acc_ref[g]
        return acc / pltpu.roll(acc, HEAD_DIM, 1)

    _combine_and_store(gl_ref, oc_ref, ow_ref, selected_of, o_ref, tq)


def _attn_prompt(q, glog, kc, vc, ov, ks, vs, kw, vw):
    b, t, d = q.shape
    tq = 128
    n_win = WINDOW // tq + 1
    ncp = kc.shape[1]
    nbp = ov.shape[1]
    blk = lambda w: pl.BlockSpec((1, tq, w), lambda bi, i: (bi, i, 0))
    per_b = lambda a: pl.BlockSpec((1,) + a.shape[1:], lambda bi, i: (bi, 0, 0))
    win = [pl.BlockSpec((1, tq, KV_WIDTH),
                        functools.partial(lambda bi, i, k: (bi, jnp.maximum(i - (n_win - 1) + k, 0), 0), k=k))
           for k in range(n_win)]
    rows = GROUP_R * tq
    return pl.pallas_call(
        functools.partial(_attn_prompt_kernel, tq=tq, n_win=n_win),
        grid=(b, t // tq),
        in_specs=[blk(d), blk(LANES), per_b(kc), per_b(vc), pl.BlockSpec(ov.shape, lambda bi, i: (0, 0)),
                  per_b(ks), per_b(vs)] + win + win,
        out_specs=blk(d),
        out_shape=jax.ShapeDtypeStruct((b, t, d), F32),
        scratch_shapes=_attn_scratch(rows, tq, nbp),
        compiler_params=_cparams(("arbitrary", "arbitrary")), name="attn_prompt",
    )(q, glog, kc, vc, ov, ks, vs, *([kw] * n_win), *([vw] * n_win))


def _attn_sample_kernel(pt_ref, q_ref, gl_ref, kc_ref, vc_ref, ov_ref, kw_ref, vw_ref, kn_ref, vn_ref,
                        poolk_ref, poolv_ref, o_ref,
                        bufk, bufv, sems, q4_ref, oc_ref, ow_ref, selneg_ref, qbd_ref, m_ref, l_ref, acc_ref,
                        *, tq, past, n_chunks):
    b, c = pl.program_id(0), pl.program_id(1)
    q0 = past
    n_blocks = past // SEL_LEN + 1
    rows = GROUP_R * tq

    @pl.when(c == 0)
    def _():
        _build_q4(q_ref, q4_ref, tq)
        for g in range(N_KV):
            own, zero = q4_ref[g], jnp.zeros((rows, LANES), q4_ref.dtype)
            qbd_ref[g * rows:(g + 1) * rows, :] = jnp.concatenate([zero, own] if g // 2 else [own, zero], axis=1)
        _compressed_and_select(q4_ref, kc_ref, vc_ref, ov_ref, oc_ref, selneg_ref, tq, q0, n_blocks)
        lw = kw_ref.shape[1]
        kpos = past - WINDOW + lax.broadcasted_iota(jnp.int32, (tq, lw), 1)
        qpos = _row_qpos(tq, lw, tq, q0)
        in_window = (kpos <= qpos) & (kpos >= qpos - WINDOW)
        _window_attend(q4_ref, kw_ref[0], vw_ref[0], jnp.where(in_window, 0.0, NEG), ow_ref, tq)
        m_ref[...] = jnp.full(m_ref.shape, NEG, F32)
        l_ref[...] = jnp.zeros(l_ref.shape, F32)
        acc_ref[...] = jnp.zeros(acc_ref.shape, F32)

    def tile_update(k, v, feature_major, first_block, key0):
        kt = SAMPLE_KEY_TILE
        selneg = selneg_ref[...]
        nb = selneg.shape[1]
        jj = lax.broadcasted_iota(jnp.int32, (nb, kt), 0)
        kk = lax.broadcasted_iota(jnp.int32, (nb, kt), 1)
        expand = jnp.where(jj == first_block + (kk >> SEL_SHIFT), 1.0, 0.0).astype(MXU_DT)
        bias_sel = jnp.dot(selneg, expand, preferred_element_type=F32)
        kpos = key0 + lax.broadcasted_iota(jnp.int32, (tq, kt), 1)
        causal = jnp.where(kpos <= _row_qpos(tq, kt, tq, q0), 0.0, NEG)
        bias = bias_sel.reshape(N_KV, 1, tq, kt) + causal[None, None]
        qbd = qbd_ref[...]
        s = jnp.dot(qbd, k, preferred_element_type=F32) if feature_major else _dot_nt(qbd, k)
        t = (s.reshape(N_KV, GROUP_R, tq, kt) + bias).reshape(N_KV * rows, kt)
        m_old = m_ref[...]
        m_new = jnp.maximum(m_old, jnp.max(t, axis=1, keepdims=True))
        alpha = jnp.exp(m_old - m_new)
        e = jnp.concatenate([jnp.exp(t[:, i * LANES:(i + 1) * LANES] - m_new) for i in range(kt // LANES)],
                            axis=1)
        l_ref[...] = alpha * l_ref[...] + jnp.sum(e, axis=1, keepdims=True)
        e = e.astype(MXU_DT)
        pv = _dot_nt(e, v) if feature_major else jnp.dot(e, v, preferred_element_type=F32)
        acc_ref[...] = jnp.concatenate([alpha] * (KV_WIDTH // LANES), axis=1) * acc_ref[...] + pv
        m_ref[...] = m_new

    @pl.when(c < n_chunks)
    def _():
        _gather_pages(pt_ref, b, c * PAGES_PER_STEP, (poolk_ref, poolv_ref), (bufk, bufv), sems)
        pages_per_tile = SAMPLE_KEY_TILE // PAGE_SIZE

        def body(kt, carry):
            p0 = kt * pages_per_tile
            k_t = jnp.concatenate([bufk[p0 + j] for j in range(pages_per_tile)], axis=1).astype(MXU_DT)
            v_t = jnp.concatenate([bufv[p0 + j] for j in range(pages_per_tile)], axis=1).astype(MXU_DT)
            page0 = c * PAGES_PER_STEP + p0
            tile_update(k_t, v_t, True, page0 * (PAGE_SIZE // SEL_LEN), page0 * PAGE_SIZE)
            return carry

        lax.fori_loop(0, PAGES_PER_STEP // pages_per_tile, body, 0)

    @pl.when(c == n_chunks)
    def _():
        tile_update(kn_ref[0], vn_ref[0], False, past // SEL_LEN, past)
        inv = 1.0 / l_ref[...]

        def selected_of(g):
            sl = slice((g // 2) * LANES, (g // 2 + 1) * LANES)
            return acc_ref[g * rows:(g + 1) * rows, sl] * inv[g * rows:(g + 1) * rows]

        _combine_and_store(gl_ref, oc_ref, ow_ref, selected_of, o_ref, tq)


def _attn_sample(page_table, q, glog, kc, vc, ov, kw_full, vw_full, k_new, v_new, poolk, poolv):
    b, tq, d = q.shape
    n_pages = page_table.shape[1]
    past = n_pages * PAGE_SIZE
    n_chunks = n_pages // PAGES_PER_STEP
    rows = GROUP_R * tq
    nbp = ov.shape[1]
    per_b = lambda a: pl.BlockSpec((1,) + a.shape[1:], lambda bi, c, pt: (bi, 0, 0))
    any_spec = pl.BlockSpec(memory_space=pl.ANY)
    grid_spec = pltpu.PrefetchScalarGridSpec(
        num_scalar_prefetch=1, grid=(b, n_chunks + 1),
        in_specs=[per_b(q), per_b(glog), per_b(kc), per_b(vc),
                  pl.BlockSpec(ov.shape, lambda bi, c, pt: (0, 0)),
                  per_b(kw_full), per_b(vw_full), per_b(k_new), per_b(v_new), any_spec, any_spec],
        out_specs=per_b(q),
        scratch_shapes=[pltpu.VMEM((PAGES_PER_STEP, KV_WIDTH, PAGE_SIZE), F32),
                        pltpu.VMEM((PAGES_PER_STEP, KV_WIDTH, PAGE_SIZE), F32),
                        pltpu.SemaphoreType.DMA((2,))] + _attn_scratch(rows, tq, nbp)[:4]
                       + [pltpu.VMEM((N_KV * rows, KV_WIDTH), MXU_DT), pltpu.VMEM((N_KV * rows, LANES), F32),
                          pltpu.VMEM((N_KV * rows, LANES), F32), pltpu.VMEM((N_KV * rows, KV_WIDTH), F32)])
    return pl.pallas_call(
        functools.partial(_attn_sample_kernel, tq=tq, past=past, n_chunks=n_chunks),
        grid_spec=grid_spec,
        out_shape=jax.ShapeDtypeStruct((b, tq, d), F32),
        compiler_params=_cparams(("arbitrary", "arbitrary")), name="attn_sample",
    )(page_table, q, glog, kc, vc, ov, kw_full, vw_full, k_new, v_new, poolk, poolv)


def _win_update_kernel(ck_ref, cv_ref, nk_ref, nv_ref, ok_ref, ov_ref, fk_ref, fv_ref):
    t = nk_ref.shape[1]
    wb = ck_ref.shape[1]
    for c_ref, n_ref, o_ref, f_ref in ((ck_ref, nk_ref, ok_ref, fk_ref), (cv_ref, nv_ref, ov_ref, fv_ref)):
        o_ref[0, 0:wb - t, :] = c_ref[0, t:wb, :]
        o_ref[0, wb - t:wb, :] = n_ref[0]
        pad = jnp.zeros((f_ref.shape[1] - wb - t, f_ref.shape[2]), F32)
        f_ref[0] = jnp.concatenate([c_ref[0], n_ref[0], pad], axis=0).astype(f_ref.dtype)


def _win_update(cache_k, cache_v, new_k, new_v):
    b, wb, w = cache_k.shape
    t = new_k.shape[1]
    full_len = wb + LANES
    cs = pl.BlockSpec((1, wb, w), lambda i: (i, 0, 0))
    ns = pl.BlockSpec((1, t, w), lambda i: (i, 0, 0))
    fs = pl.BlockSpec((1, full_len, w), lambda i: (i, 0, 0))
    return pl.pallas_call(
        _win_update_kernel, grid=(b,), in_specs=[cs, cs, ns, ns], out_specs=[cs, cs, fs, fs],
        out_shape=[jax.ShapeDtypeStruct((b, wb, w), F32)] * 2
                  + [jax.ShapeDtypeStruct((b, full_len, w), MXU_DT)] * 2,
        compiler_params=_cparams(("arbitrary",)), name="win_update",
    )(cache_k, cache_v, new_k, new_v)


def _overlap_matrix(n_cmp_pad, n_cmp, n_blocks_pad, n_blocks):
    n = jnp.arange(n_cmp_pad)[:, None]
    j = jnp.arange(n_blocks_pad)[None, :]
    c_start, s_start = n * CMP_STRIDE, j * SEL_LEN
    ov = (c_start < s_start + SEL_LEN) & (c_start + CMP_LEN > s_start) & (n < n_cmp) & (j < n_blocks)
    return ov.astype(MXU_DT)


def _finish_rows(a, w_ref, gpost_ref, x_ref, gate_ref, y_ref):
    o = jnp.dot(a.astype(MXU_DT), w_ref[...], preferred_element_type=F32)
    n = o * lax.rsqrt(jnp.mean(o * o, axis=-1, keepdims=True) + RMS_EPS) * gpost_ref[...]
    y_ref[0] = x_ref[0] + gate_ref[0] * n


def _attn_out_kernel(o_ref, z_ref, w_ref, gpost_ref, x_ref, gate_ref, y_ref):
    _finish_rows(o_ref[0] * _silu(z_ref[0]), w_ref, gpost_ref, x_ref, gate_ref, y_ref)


def _ssm_out_kernel(y_ref, z_ref, wglu_ref, w_ref, gpost_ref, x_ref, gate_ref, o_ref):
    v = jnp.dot(y_ref[0].astype(MXU_DT), wglu_ref[...], preferred_element_type=F32)
    half = v.shape[1] // 2
    glu = v[:, :half] * _sigmoid(v[:, half:])
    _finish_rows(glu * _silu(z_ref[0]), w_ref, gpost_ref, x_ref, gate_ref, o_ref)


def _mixer_out(body, name, acts, weights, g_post, x, gate):
    b, t, d = x.shape
    tm = min(t, 256)
    row = pl.BlockSpec((1, tm, d), lambda i, j: (i, j, 0))
    full = lambda a: pl.BlockSpec(a.shape, lambda i, j: (0,) * a.ndim)
    per_b = pl.BlockSpec((1, 1, d), lambda i, j: (i, 0, 0))
    return pl.pallas_call(
        body, grid=(b, t // tm),
        in_specs=[row] * len(acts) + [full(w) for w in weights] + [full(g_post), row, per_b],
        out_specs=row, out_shape=jax.ShapeDtypeStruct((b, t, d), F32),
        compiler_params=_cparams(("arbitrary", "arbitrary")), name=name,
    )(*acts, *weights, g_post, x, gate)


def _ssm_proj_kernel(x_ref, shift_ref, scale_ref, g_ref, w_ref, u_ref, z_ref):
    h = _rms_mod(x_ref[0], g_ref[...], scale_ref[0], shift_ref[0]).astype(MXU_DT)
    p = jnp.dot(h, w_ref[...], preferred_element_type=F32)
    half = p.shape[1] // 2
    u_ref[0] = p[:, :half]
    z_ref[0] = p[:, half:]


def _ssm_proj(x, shift, scale, g_pre, w):
    b, t, d = x.shape
    tm = min(t, 256)
    row = pl.BlockSpec((1, tm, d), lambda i, j: (i, j, 0))
    per_b = pl.BlockSpec((1, 1, d), lambda i, j: (i, 0, 0))
    full = lambda a: pl.BlockSpec(a.shape, lambda i, j: (0,) * a.ndim)
    out = jax.ShapeDtypeStruct((b, t, d), F32)
    return pl.pallas_call(
        _ssm_proj_kernel, grid=(b, t // tm),
        in_specs=[row, per_b, per_b, full(g_pre), full(w)], out_specs=[row, row], out_shape=[out, out],
        compiler_params=_cparams(("arbitrary", "arbitrary")), name="ssm_proj",
    )(x, shift, scale, g_pre, w)


def _s5_prep_kernel(ar_ref, ai_ref, ldt_ref, bre_ref, bim_ref, lre_ref, lim_ref, cbre_ref, cbim_ref):
    ar, ai = ar_ref[...], ai_ref[...]
    dt = jnp.exp(ldt_ref[...])
    mag = jnp.exp(dt * ar)
    lam_re, lam_im = mag * jnp.cos(dt * ai), mag * jnp.sin(dt * ai)
    den = ar * ar + ai * ai
    nr, ni = lam_re - 1.0, lam_im
    coef_re, coef_im = (nr * ar + ni * ai) / den, (ni * ar - nr * ai) / den
    lre_ref[...] = lam_re
    lim_ref[...] = lam_im
    cr, ci = coef_re[:, None, :], coef_im[:, None, :]
    b_re, b_im = bre_ref[...], bim_ref[...]
    cbre_ref[...] = cr * b_re - ci * b_im
    cbim_ref[...] = cr * b_im + ci * b_re


def _s5_prep(a_re, a_im, log_dt, b_re_t, b_im_t):
    ng, p = a_re.shape
    full = lambda a: pl.BlockSpec(a.shape, lambda: (0,) * a.ndim)
    args = (a_re, a_im, log_dt.reshape(ng, 1), b_re_t, b_im_t)
    lam = jax.ShapeDtypeStruct((ng, p), F32)
    cb = jax.ShapeDtypeStruct(b_re_t.shape, F32)
    return pl.pallas_call(
        _s5_prep_kernel, in_specs=[full(a) for a in args],
        out_specs=[full(lam), full(lam), full(cb), full(cb)], out_shape=[lam, lam, cb, cb],
        name="s5_prep",
    )(*args)


SCAN_LANES = 512


def _s5_scan_kernel(u_ref, d_ref, bre_ref, bim_ref, cre_ref, cim_ref, lre_ref, lim_ref, h0re_ref, h0im_ref,
                    y_ref, hre_ref, him_ref, sre, sim, *, n_batch, nbp, tt):
    ti = pl.program_id(1)
    n_lc = sre.shape[0]

    @pl.when(ti == 0)
    def _():
        hre_ref[...] = h0re_ref[...]
        him_ref[...] = h0im_ref[...]
        if nbp != n_batch:
            sre[...] = jnp.zeros(sre.shape, F32)
            sim[...] = jnp.zeros(sim.shape, F32)

    def drive(b, carry):
        ub = u_ref[b].astype(MXU_DT)
        for s_ref, w_ref in ((sre, bre_ref), (sim, bim_ref)):
            bb = jnp.dot(ub, w_ref[0], preferred_element_type=F32)
            for c in range(n_lc):
                s_ref[c, pl.ds(b, tt, stride=nbp), :] = bb[:, c * LANES:(c + 1) * LANES]
        return carry

    lax.fori_loop(0, n_batch, drive, 0)

    per_scan = SCAN_LANES // LANES
    for sb in range(nbp // SUBLANES):
        for lc in range(n_lc // per_scan):
            chunks = range(lc * per_scan, (lc + 1) * per_scan)
            srow = slice(sb * SUBLANES, (sb + 1) * SUBLANES)
            lane = lambda c: slice(c * LANES, (c + 1) * LANES)
            lam_re = [jnp.broadcast_to(lre_ref[:, lane(c)], (SUBLANES, LANES)) for c in chunks]
            lam_im = [jnp.broadcast_to(lim_ref[:, lane(c)], (SUBLANES, LANES)) for c in chunks]

            def step(t, carry):
                r0 = pl.multiple_of(t * nbp + sb * SUBLANES, SUBLANES)
                new = []
                for k, c in enumerate(chunks):
                    h_re, h_im = carry[k]
                    n_re = lam_re[k] * h_re - lam_im[k] * h_im + sre[c, pl.ds(r0, SUBLANES), :]
                    n_im = lam_re[k] * h_im + lam_im[k] * h_re + sim[c, pl.ds(r0, SUBLANES), :]
                    sre[c, pl.ds(r0, SUBLANES), :] = n_re
                    sim[c, pl.ds(r0, SUBLANES), :] = n_im
                    new.append((n_re, n_im))
                return tuple(new)

            init = tuple((hre_ref[srow, lane(c)], him_ref[srow, lane(c)]) for c in chunks)
            final = lax.fori_loop(0, tt, step, init, unroll=min(tt, 8))
            for k, c in enumerate(chunks):
                hre_ref[srow, lane(c)] = final[k][0]
                him_ref[srow, lane(c)] = final[k][1]

    def readout(b, carry):
        rows = lambda s_ref: jnp.concatenate(
            [s_ref[c, pl.ds(b, tt, stride=nbp), :] for c in range(n_lc)], axis=1).astype(MXU_DT)
        y = (jnp.dot(rows(sre), cre_ref[0], preferred_element_type=F32)
             - jnp.dot(rows(sim), cim_ref[0], preferred_element_type=F32))
        y_ref[b] = y + d_ref[...] * u_ref[b]
        return carry

    lax.fori_loop(0, n_batch, readout, 0)


def _s5_scan(u, d_skip, b_re_bd, b_im_bd, c_re_bd, c_im_bd, lam_re, lam_im, h0_re, h0_im, nbp):
    b, t, w = u.shape
    q, wq, sq = b_re_bd.shape
    ns = q * sq
    tt = min(t, 256)
    ublk = pl.BlockSpec((b, tt, wq), lambda k, i: (0, i, k))
    dblk = pl.BlockSpec((1, wq), lambda k, i: (0, k))
    bblk = pl.BlockSpec((1, wq, sq), lambda k, i: (k, 0, 0))
    cblk = pl.BlockSpec((1, sq, wq), lambda k, i: (k, 0, 0))
    lblk = pl.BlockSpec((1, sq), lambda k, i: (0, k))
    hblk = pl.BlockSpec((nbp, sq), lambda k, i: (0, k))
    st = jax.ShapeDtypeStruct((nbp, ns), F32)
    return pl.pallas_call(
        functools.partial(_s5_scan_kernel, n_batch=b, nbp=nbp, tt=tt),
        grid=(q, t // tt),
        in_specs=[ublk, dblk, bblk, bblk, cblk, cblk, lblk, lblk, hblk, hblk],
        out_specs=[ublk, hblk, hblk],
        out_shape=[jax.ShapeDtypeStruct((b, t, w), F32), st, st],
        scratch_shapes=[pltpu.VMEM((sq // LANES, tt * nbp, LANES), F32)] * 2,
        compiler_params=_cparams(("arbitrary", "arbitrary")), name="s5_scan",
    )(u, d_skip, b_re_bd, b_im_bd, c_re_bd, c_im_bd, lam_re, lam_im, h0_re, h0_im)


def _s5_block_diag(cb_re, cb_im, c_re, c_im):
    ng, c, p = cb_re.shape
    gq = ng // S5_QUARTERS
    eye = jnp.eye(gq, dtype=F32)

    def drive(w):
        w4 = w.reshape(S5_QUARTERS, gq, c, p)
        return jnp.einsum('qgcp,gk->qgckp', w4, eye).reshape(S5_QUARTERS, gq * c, gq * p).astype(MXU_DT)

    def readout(w):
        w4 = w.reshape(S5_QUARTERS, gq, c, p)
        return jnp.einsum('qgcp,gk->qgpkc', w4, eye).reshape(S5_QUARTERS, gq * p, gq * c).astype(MXU_DT)

    return drive(cb_re), drive(cb_im), readout(c_re), readout(c_im)


def _nsa_layer(xp, xs, modp, mods, caches, page_table, g_pre, g_post, w_in, pe, wk1, wk2, wv1, wv2, w_out):
    cache_kc, cache_vc, cache_ks, cache_vs, cache_kw, cache_vw = caches
    b, t, d = xp.shape
    sb, st, _ = xs.shape
    n_pages = page_table.shape[1]
    past = n_pages * PAGE_SIZE
    assert st < CMP_STRIDE and n_pages % PAGES_PER_STEP == 0 and t % KEY_TILE == 0

    aw = N_HEADS * HEAD_DIM
    cuts = [aw, aw + 6 * KV_WIDTH, aw + 6 * KV_WIDTH + 3 * N_HEADS]
    wq = w_in[:, :cuts[0]].astype(MXU_DT)
    wkv = w_in[:, cuts[0]:cuts[1]].astype(MXU_DT)
    wg = jnp.pad(w_in[:, cuts[1]:cuts[2]], ((0, 0), (0, LANES - 3 * N_HEADS))).astype(MXU_DT)
    wz = w_in[:, cuts[2]:].astype(MXU_DT)
    g_pre2, g_post2 = g_pre.reshape(1, d), g_post.reshape(1, d)

    proj_p = _attn_proj(xp, modp[0], modp[1], g_pre2, _rope_tables(jnp.arange(t)), wq, wkv, wg, wz)
    proj_s = _attn_proj(xs, mods[0], mods[1], g_pre2, _rope_tables(past + jnp.arange(st)), wq, wkv, wg, wz)
    (q_p, kc_p, vc_p, ks_p, vs_p, kw_p, vw_p, ksb_p, vsb_p, kwb_p, vwb_p, gl_p, z_p) = proj_p
    (q_s, kc_s, vc_s, ks_s, vs_s, kw_s, vw_s, ksb_s, vsb_s, _, _, gl_s, z_s) = proj_s

    wk_big, w1k_t, w2k_bd = _cmp_weights(wk1, wk2)
    wv_big, w1v_t, w2v_bd = _cmp_weights(wv1, wv2)
    pe_col = pe.reshape(CMP_LEN * HEAD_DIM, 1)
    chunk_w = CMP_STRIDE * KV_WIDTH

    n_ch = t // CMP_STRIDE
    pk, pv = _cmp_part(kc_p.reshape(b * n_ch, chunk_w), vc_p.reshape(b * n_ch, chunk_w), wk_big, wv_big)
    kcc_p, vcc_p = _cmp_finish(pk.reshape(b, n_ch, -1), pv.reshape(b, n_ch, -1), pe_col,
                               w1k_t, w1v_t, w2k_bd, w2v_bd)
    ns_p = -(-t // SEL_LEN)
    nbp_p = -(-ns_p // LANES) * LANES
    ov_p = _overlap_matrix(n_ch, n_ch - CMP_RATIO + 1, nbp_p, ns_p)
    o_p = _attn_prompt(q_p, gl_p, kcc_p, vcc_p, ov_p, ksb_p, vsb_p, kwb_p, vwb_p)
    y_p = _mixer_out(_attn_out_kernel, "attn_out", (o_p, z_p), (w_out.astype(MXU_DT),), g_post2, xp, modp[2])

    cpp = PAGE_SIZE // CMP_STRIDE
    n_phys = cache_kc.shape[0]
    feature_major = lambda pool: pool.transpose(0, 2, 3, 1).reshape(n_phys, KV_WIDTH, PAGE_SIZE)
    pk, pv = _cmp_part_paged(page_table, feature_major(cache_kc), feature_major(cache_vc), wk_big, wv_big)
    kcc_s, vcc_s = _cmp_finish(pk, pv, pe_col, w1k_t, w1v_t, w2k_bd, w2v_bd)
    n_ch_s = (past + st) // CMP_STRIDE
    ns_s = -(-(past + st) // SEL_LEN)
    nbp_s = -(-ns_s // LANES) * LANES
    ov_s = _overlap_matrix(n_pages * cpp, n_ch_s - CMP_RATIO + 1, nbp_s, ns_s)
    wb = cache_kw.shape[1]
    win_k, win_v, kw_full, vw_full = _win_update(cache_kw.reshape(sb, wb, KV_WIDTH),
                                                 cache_vw.reshape(sb, wb, KV_WIDTH), kw_s, vw_s)
    pad_new = lambda a: jnp.pad(a, ((0, 0), (0, SAMPLE_KEY_TILE - st), (0, 0)))
    o_s = _attn_sample(page_table, q_s, gl_s, kcc_s, vcc_s, ov_s, kw_full, vw_full,
                       pad_new(ksb_s), pad_new(vsb_s), feature_major(cache_ks), feature_major(cache_vs))
    y_s = _mixer_out(_attn_out_kernel, "attn_out_s", (o_s, z_s), (w_out.astype(MXU_DT),), g_post2, xs, mods[2])

    heads = lambda a: a.reshape(a.shape[0], a.shape[1], N_KV, HEAD_DIM)
    wl = min(WINDOW, t)
    rows_p = (kc_p, vc_p, ks_p, vs_p, kw_p[:, t - wl:], vw_p[:, t - wl:])
    rows_s = (kc_s, vc_s, ks_s, vs_s, win_k, win_v)
    return y_p, y_s, [heads(a) for a in rows_p], [heads(a) for a in rows_s]


def _s5_layer(xp, xs, modp, mods, state_re, state_im, g_pre, g_post, w_in, a_re, a_im, log_dt,
              b_re, b_im, c_re, c_im, d_skip, w_glu, w_out):
    b, t, d = xp.shape
    sb, st, _ = xs.shape
    ng, p, c = b_re.shape
    g_pre2, g_post2 = g_pre.reshape(1, d), g_post.reshape(1, d)
    w_in16, w_glu16, w_out16 = w_in.astype(MXU_DT), w_glu.astype(MXU_DT), w_out.astype(MXU_DT)
    lam_re, lam_im, cb_re, cb_im = _s5_prep(a_re, a_im, log_dt, b_re.transpose(0, 2, 1), b_im.transpose(0, 2, 1))
    bd = _s5_block_diag(cb_re, cb_im, c_re, c_im)
    lam = (lam_re.reshape(1, ng * p), lam_im.reshape(1, ng * p))
    d2 = d_skip.reshape(1, d)

    outs = []
    for x, mod, h0 in ((xp, modp, None), (xs, mods, (state_re, state_im))):
        nb = x.shape[0]
        nbp = -(-nb // SUBLANES) * SUBLANES
        u, z = _ssm_proj(x, mod[0], mod[1], g_pre2, w_in16)
        if h0 is None:
            h0_re = h0_im = jnp.zeros((nbp, ng * p), F32)
        else:
            h0_re, h0_im = (jnp.pad(h.reshape(nb, ng * p), ((0, nbp - nb), (0, 0))) for h in h0)
        y, h_re, h_im = _s5_scan(u, d2, *bd, *lam, h0_re, h0_im, nbp)
        out = _mixer_out(_ssm_out_kernel, "ssm_out", (y, z), (w_glu16, w_out16), g_post2, x, mod[2])
        outs.append((out, h_re[:nb].reshape(nb, ng, p), h_im[:nb].reshape(nb, ng, p)))
    return outs


def kernel(x_prompt, x_sample, cache_k_cmp, cache_v_cmp, cache_k_sel, cache_v_sel, cache_k_win, cache_v_win, state_s5_re, state_s5_im, page_table, c_prompt, c_sample, norm_pre, norm_post, w_ada, b_ada, w_in_attn, pe_cmp, w_cmp_k1, w_cmp_k2, w_cmp_v1, w_cmp_v2, w_out_attn, w_in_ssm, s5_a_re, s5_a_im, s5_log_dt, s5_b_re, s5_b_im, s5_c_re, s5_c_im, s5_d, w_glu, w_out_ssm):
    b, t, d = x_prompt.shape
    sb = x_sample.shape[0]
    depth = w_ada.shape[0]
    c_all = jnp.concatenate([c_prompt, c_sample], axis=0)
    n_c = -(-c_all.shape[0] // SUBLANES) * SUBLANES
    mod = _ada(jnp.pad(c_all, ((0, n_c - c_all.shape[0]), (0, 0))), w_ada, b_ada)

    yp, ys = x_prompt, x_sample
    attn_p, attn_s, ssm_p, ssm_s = [], [], [], []
    for i in range(depth):
        parts = [mod[i, :, k * d:(k + 1) * d] for k in range(3)]
        modp = [a[:b].reshape(b, 1, d) for a in parts]
        mods = [a[b:b + sb].reshape(sb, 1, d) for a in parts]
        li = i // 2
        if i % 2 == 0:
            caches = (cache_k_cmp[li], cache_v_cmp[li], cache_k_sel[li], cache_v_sel[li],
                      cache_k_win[li], cache_v_win[li])
            yp, ys, rows_p, rows_s = _nsa_layer(
                yp, ys, modp, mods, caches, page_table, norm_pre[i], norm_post[i], w_in_attn[li], pe_cmp[li],
                w_cmp_k1[li], w_cmp_k2[li], w_cmp_v1[li], w_cmp_v2[li], w_out_attn[li])
            attn_p.append(rows_p)
            attn_s.append(rows_s)
        else:
            (yp, pr, pi), (ys, sr, si) = _s5_layer(
                yp, ys, modp, mods, state_s5_re[li], state_s5_im[li], norm_pre[i], norm_post[i], w_in_ssm[li],
                s5_a_re[li], s5_a_im[li], s5_log_dt[li], s5_b_re[li], s5_b_im[li], s5_c_re[li], s5_c_im[li],
                s5_d[li], w_glu[li], w_out_ssm[li])
            ssm_p.append((pr, pi))
            ssm_s.append((sr, si))

    outs = [yp, ys]
    for n in range(6):
        outs.append(jnp.stack([rows[n] for rows in attn_p]))
        outs.append(jnp.stack([rows[n] for rows in attn_s]))
    for n in range(2):
        outs.append(jnp.stack([st[n] for st in ssm_p]))
        outs.append(jnp.stack([st[n] for st in ssm_s]))
    return tuple(outs)
```

```python
import functools
import math

import jax
import jax.numpy as jnp
from jax import lax
from jax.experimental import pallas as pl
from jax.experimental.pallas import tpu as pltpu

N_HEADS = 16
HEAD_DIM = 64
N_KV = 4
GROUP_R = N_HEADS // N_KV
ROT_HALF = HEAD_DIM // 8
ROPE_THETA = 500000.0
CMP_LEN = 32
CMP_STRIDE = 16
CMP_RATIO = CMP_LEN // CMP_STRIDE
CMP_HIDDEN = 64
SEL_LEN = 64
SEL_SHIFT = SEL_LEN.bit_length() - 1
N_SEL = 16
WINDOW = 512
PAGE_SIZE = 128
S5_GROUP = 16
S5_STATE = 64
NEG = -1e30
LOG2_E = math.log2(math.e)
SEL_FORCE = 1e4
RMS_EPS = 1e-6
KV_WIDTH = N_KV * HEAD_DIM

LANES = 128
SUBLANES = 8
VMEM_LIMIT_BYTES = 56 * 1024 * 1024

MXU_DT = jnp.bfloat16
F32 = jnp.float32

PROMPT_Q_TILE = 256
KEY_TILE = 256
BLOCKS_PER_TILE = KEY_TILE // SEL_LEN
SAMPLE_KEY_TILE = 1024
PAGES_PER_STEP = 32
S5_QUARTERS = 4


def _cparams(sem):
    return pltpu.CompilerParams(dimension_semantics=sem, vmem_limit_bytes=VMEM_LIMIT_BYTES)


def _dot(a, b):
    return jnp.dot(a.astype(MXU_DT), b.astype(MXU_DT), preferred_element_type=F32)


def _dot_nt(a, b):
    return lax.dot_general(a.astype(MXU_DT), b.astype(MXU_DT), (((1,), (1,)), ((), ())),
                           preferred_element_type=F32)


def _split3(x):
    hi = x.astype(MXU_DT)
    r1 = x - hi.astype(F32)
    mid = r1.astype(MXU_DT)
    lo = (r1 - mid.astype(F32)).astype(MXU_DT)
    return hi, mid, lo


def _dot_exact_rhs(x, m):
    hi, mid, lo = _split3(x)
    return (jnp.dot(hi, m, preferred_element_type=F32) + jnp.dot(mid, m, preferred_element_type=F32)
            + jnp.dot(lo, m, preferred_element_type=F32))


def _dot_f32(a, b):
    ah, am, _ = _split3(a)
    bh, bm, _ = _split3(b)
    d = lambda p, q: jnp.dot(p, q, preferred_element_type=F32)
    return d(ah, bh) + (d(ah, bm) + d(am, bh)) + d(am, bm)


def _sigmoid(x):
    return 1.0 / (1.0 + jnp.exp(-x))


def _silu(x):
    return x * _sigmoid(x)


def _ada_kernel(c_ref, w_ref, b_ref, o_ref):
    o_ref[...] = _dot_f32(c_ref[...], w_ref[...]) + b_ref[...]


def _ada(c_all, w_ada, b_ada):
    depth, d, d3 = w_ada.shape
    n = c_all.shape[0]
    return pl.pallas_call(
        _ada_kernel,
        grid=(depth, d3 // d),
        in_specs=[pl.BlockSpec((n, d), lambda i, j: (0, 0)),
                  pl.BlockSpec((None, d, d), lambda i, j: (i, 0, j)),
                  pl.BlockSpec((None, 1, d), lambda i, j: (i, 0, j))],
        out_specs=pl.BlockSpec((None, n, d), lambda i, j: (i, 0, j)),
        out_shape=jax.ShapeDtypeStruct((depth, n, d3), F32),
        compiler_params=_cparams(("arbitrary", "arbitrary")),
        name="ada_mod",
    )(c_all, w_ada, b_ada.reshape(depth, 1, d3))


def _rms_mod(x, g, scale, shift):
    y = x * lax.rsqrt(jnp.mean(x * x, axis=-1, keepdims=True) + RMS_EPS)
    return (y * g) * (1.0 + scale) + shift


def _rope_slab(x, c, s1, s2):
    return x * c + pltpu.roll(x, ROT_HALF, 1) * s1 + pltpu.roll(x, LANES - ROT_HALF, 1) * s2


def _attn_proj_kernel(x_ref, shift_ref, scale_ref, g_ref, c_ref, s1_ref, s2_ref,
                      wq_ref, wkv_ref, wg_ref, wz_ref,
                      q_ref, kc_ref, vc_ref, ks_ref, vs_ref, kw_ref, vw_ref,
                      ksb_ref, vsb_ref, kwb_ref, vwb_ref, gl_ref, z_ref):
    h = _rms_mod(x_ref[0], g_ref[...], scale_ref[0], shift_ref[0]).astype(MXU_DT)
    c, s1, s2 = c_ref[...], s1_ref[...], s2_ref[...]
    q = jnp.dot(h, wq_ref[...], preferred_element_type=F32)
    qscale = HEAD_DIM ** -0.5 * LOG2_E
    for j in range(q.shape[1] // LANES):
        sl = slice(j * LANES, (j + 1) * LANES)
        q_ref[0, :, sl] = (_rope_slab(q[:, sl], c, s1, s2) * qscale).astype(q_ref.dtype)
    kv = jnp.dot(h, wkv_ref[...], preferred_element_type=F32)
    outs = ((kc_ref, None, False), (vc_ref, None, False), (ks_ref, ksb_ref, True),
            (vs_ref, vsb_ref, False), (kw_ref, kwb_ref, True), (vw_ref, vwb_ref, False))
    for n, (o_ref, ob_ref, roped) in enumerate(outs):
        for j in range(KV_WIDTH // LANES):
            lo = n * KV_WIDTH + j * LANES
            v = kv[:, lo:lo + LANES]
            if roped:
                v = _rope_slab(v, c, s1, s2)
            o_ref[0, :, j * LANES:(j + 1) * LANES] = v
            if ob_ref is not None:
                ob_ref[0, :, j * LANES:(j + 1) * LANES] = v.astype(ob_ref.dtype)
    gl_ref[0] = jnp.dot(h, wg_ref[...], preferred_element_type=F32)
    z_ref[0] = jnp.dot(h, wz_ref[...], preferred_element_type=F32)


def _attn_proj(x, shift, scale, g_pre, tables, wq, wkv, wg, wz):
    b, t, d = x.shape
    tm = min(t, 256)
    cos_t, s1_t, s2_t = tables
    row = lambda i, j: (i, j, 0)
    per_b = pl.BlockSpec((1, 1, d), lambda i, j: (i, 0, 0))
    full = lambda a: pl.BlockSpec(a.shape, lambda i, j: (0,) * a.ndim)
    tab = pl.BlockSpec((tm, LANES), lambda i, j: (j, 0))
    kv32 = jax.ShapeDtypeStruct((b, t, KV_WIDTH), F32)
    kv16 = jax.ShapeDtypeStruct((b, t, KV_WIDTH), MXU_DT)
    kvspec = pl.BlockSpec((1, tm, KV_WIDTH), row)
    return pl.pallas_call(
        _attn_proj_kernel,
        grid=(b, t // tm),
        in_specs=[pl.BlockSpec((1, tm, d), row), per_b, per_b, full(g_pre), tab, tab, tab,
                  full(wq), full(wkv), full(wg), full(wz)],
        out_specs=[pl.BlockSpec((1, tm, d), row)] + [kvspec] * 10
                  + [pl.BlockSpec((1, tm, LANES), row), pl.BlockSpec((1, tm, d), row)],
        out_shape=[jax.ShapeDtypeStruct((b, t, d), MXU_DT)] + [kv32] * 6 + [kv16] * 4
                  + [jax.ShapeDtypeStruct((b, t, LANES), F32), jax.ShapeDtypeStruct((b, t, d), F32)],
        compiler_params=_cparams(("arbitrary", "arbitrary")),
        name="attn_proj",
    )(x, shift, scale, g_pre, cos_t, s1_t, s2_t, wq, wkv, wg, wz)


def _rope_tables(pos):
    inv = ROPE_THETA ** (-jnp.arange(ROT_HALF, dtype=F32) / ROT_HALF)
    ang = pos.astype(F32)[:, None] * inv[None]
    cos, sin = jnp.cos(ang), jnp.sin(ang)
    n = pos.shape[0]
    rest = HEAD_DIM - 2 * ROT_HALF
    zeros = lambda w: jnp.zeros((n, w), F32)
    c64 = jnp.concatenate([cos, cos, jnp.ones((n, rest), F32)], axis=1)
    s1 = jnp.concatenate([zeros(ROT_HALF), sin, zeros(rest)], axis=1)
    s2 = jnp.concatenate([-sin, zeros(ROT_HALF + rest)], axis=1)
    rep = LANES // HEAD_DIM
    return tuple(jnp.tile(a, (1, rep)) for a in (c64, s1, s2))


def _cmp_part_kernel(xk_ref, xv_ref, wk_ref, wv_ref, pk_ref, pv_ref):
    pk_ref[...] = _dot(xk_ref[...], wk_ref[...])
    pv_ref[...] = _dot(xv_ref[...], wv_ref[...])


def _cmp_part(xk, xv, wk, wv):
    rows, width = xk.shape
    tm = min(rows, 256)
    n = wk.shape[1]
    xs = pl.BlockSpec((tm, width), lambda i: (i, 0))
    ws = pl.BlockSpec(wk.shape, lambda i: (0, 0))
    os_ = pl.BlockSpec((tm, n), lambda i: (i, 0))
    return pl.pallas_call(
        _cmp_part_kernel, grid=(rows // tm,), in_specs=[xs, xs, ws, ws], out_specs=[os_, os_],
        out_shape=[jax.ShapeDtypeStruct((rows, n), F32)] * 2,
        compiler_params=_cparams(("arbitrary",)), name="cmp_part",
    )(xk, xv, wk, wv)


def _pages_dma(pt_ref, b, chunk, pools, bufs, sems, slot, wait):
    def body(p, carry):
        page = pt_ref[b, chunk * PAGES_PER_STEP + p]
        for k, (pool, buf) in enumerate(zip(pools, bufs)):
            copy = pltpu.make_async_copy(pool.at[page], buf.at[slot, p], sems.at[slot, k])
            if wait:
                copy.wait()
            else:
                copy.start()
        return carry

    lax.fori_loop(0, PAGES_PER_STEP, body, 0)


def _gather_pages_pipelined(pt_ref, b, c, n_chunks, pools, bufs, sems):
    step = b * n_chunks + c
    slot = step % 2
    more_in_batch = c + 1 < n_chunks
    next_b = jnp.where(more_in_batch, b, b + 1)
    next_c = jnp.where(more_in_batch, c + 1, 0)

    @pl.when(step == 0)
    def _():
        _pages_dma(pt_ref, b, c, pools, bufs, sems, slot, wait=False)

    @pl.when(more_in_batch | (b + 1 < pl.num_programs(0)))
    def _():
        _pages_dma(pt_ref, next_b, next_c, pools, bufs, sems, 1 - slot, wait=False)

    _pages_dma(pt_ref, b, c, pools, bufs, sems, slot, wait=True)
    return slot


def _cmp_part_paged_kernel(pt_ref, poolk_ref, poolv_ref, perm_ref, wk_ref, wv_ref, pk_ref, pv_ref,
                           bufk, bufv, xk, xv, sems):
    b, c = pl.program_id(0), pl.program_id(1)
    slot = _gather_pages_pipelined(pt_ref, b, c, pl.num_programs(1), (poolk_ref, poolv_ref),
                                   (bufk, bufv), sems)
    cpp = PAGE_SIZE // CMP_STRIDE
    perm = perm_ref[...]

    def page_rows(p, carry):
        r0 = pl.multiple_of(p * cpp, cpp)
        for buf, x in ((bufk, xk), (bufv, xv)):
            y = _dot_nt(perm, buf[slot, p])
            for s in range(CMP_STRIDE):
                x[pl.ds(r0, cpp), s * KV_WIDTH:(s + 1) * KV_WIDTH] = y[s * cpp:(s + 1) * cpp]
        return carry

    lax.fori_loop(0, PAGES_PER_STEP, page_rows, 0, unroll=4)
    pk_ref[0] = _dot(xk[...], wk_ref[...])
    pv_ref[0] = _dot(xv[...], wv_ref[...])


def _cmp_part_paged(page_table, poolk, poolv, wk, wv):
    nb, n_pages = page_table.shape
    cpp = PAGE_SIZE // CMP_STRIDE
    assert cpp == SUBLANES
    n = wk.shape[1]
    rows = PAGES_PER_STEP * cpp
    slot = jnp.arange(PAGE_SIZE)
    perm = (slot[None, :] == (slot[:, None] % cpp) * CMP_STRIDE + slot[:, None] // cpp).astype(MXU_DT)
    any_spec = pl.BlockSpec(memory_space=pl.ANY)
    full = lambda a: pl.BlockSpec(a.shape, lambda b, c, pt: (0, 0))
    os_ = pl.BlockSpec((1, rows, n), lambda b, c, pt: (b, c, 0))
    page_buf = pltpu.VMEM((2, PAGES_PER_STEP, KV_WIDTH, PAGE_SIZE), F32)
    chunk_rows = pltpu.VMEM((rows, CMP_STRIDE * KV_WIDTH), F32)
    grid_spec = pltpu.PrefetchScalarGridSpec(
        num_scalar_prefetch=1, grid=(nb, n_pages // PAGES_PER_STEP),
        in_specs=[any_spec, any_spec, full(perm), full(wk), full(wv)], out_specs=[os_, os_],
        scratch_shapes=[page_buf, page_buf, chunk_rows, chunk_rows, pltpu.SemaphoreType.DMA((2, 2))])
    return pl.pallas_call(
        _cmp_part_paged_kernel, grid_spec=grid_spec,
        out_shape=[jax.ShapeDtypeStruct((nb, n_pages * cpp, n), F32)] * 2,
        compiler_params=_cparams(("arbitrary", "arbitrary")), name="cmp_part_paged",
    )(page_table, poolk, poolv, perm, wk, wv)


def _cmp_finish_kernel(pk_ref, pv_ref, pe_ref, w1k_ref, w1v_ref, w2k_ref, w2v_ref, kc_ref, vc_ref):
    pe = pe_ref[...]
    for p_ref, w1_ref, w2_ref, o_ref in ((pk_ref, w1k_ref, w2k_ref, kc_ref),
                                         (pv_ref, w1v_ref, w2v_ref, vc_ref)):
        p = p_ref[0]
        n_ch = p.shape[0]
        p0, p1 = p[:, :KV_WIDTH], p[:, KV_WIDTH:]
        row = lax.broadcasted_iota(jnp.int32, p1.shape, 0)
        p1_next = jnp.where(row == n_ch - 1, 0.0, pltpu.roll(p1, n_ch - 1, 0))
        bias = jnp.sum(pe * w1_ref[...], axis=0, keepdims=True)
        o_ref[0] = _dot(_silu(p0 + p1_next + bias), w2_ref[...]).astype(o_ref.dtype)


def _cmp_finish(pk, pv, pe_col, w1k_t, w1v_t, w2k_bd, w2v_bd):
    nb, n_ch, n = pk.shape
    ps = pl.BlockSpec((1, n_ch, n), lambda b: (b, 0, 0))
    full = lambda a: pl.BlockSpec(a.shape, lambda b: (0,) * a.ndim)
    os_ = pl.BlockSpec((1, n_ch, KV_WIDTH), lambda b: (b, 0, 0))
    return pl.pallas_call(
        _cmp_finish_kernel, grid=(nb,),
        in_specs=[ps, ps, full(pe_col), full(w1k_t), full(w1v_t), full(w2k_bd), full(w2v_bd)],
        out_specs=[os_, os_],
        out_shape=[jax.ShapeDtypeStruct((nb, n_ch, KV_WIDTH), MXU_DT)] * 2,
        compiler_params=_cparams(("arbitrary",)), name="cmp_finish",
    )(pk, pv, pe_col, w1k_t, w1v_t, w2k_bd, w2v_bd)


def _cmp_weights(w1, w2):
    eye = jnp.eye(N_KV, dtype=F32)
    w1r = w1.reshape(CMP_RATIO, CMP_STRIDE, HEAD_DIM, CMP_HIDDEN)
    big = jnp.einsum('rsdh,gk->sgdrkh', w1r, eye).reshape(CMP_STRIDE * KV_WIDTH, CMP_RATIO * KV_WIDTH)
    w1_t = jnp.tile(w1, (1, N_KV))
    w2_bd = jnp.einsum('hd,gk->ghkd', w2, eye).reshape(N_KV * CMP_HIDDEN, KV_WIDTH)
    return big.astype(MXU_DT), w1_t, w2_bd.astype(MXU_DT)


def _lane_half_mask(shape, half):
    lane = lax.broadcasted_iota(jnp.int32, shape, len(shape) - 1)
    return (lane >= HEAD_DIM) if half else (lane < HEAD_DIM)


def _build_q4(q_ref, q4_ref, tq):
    del tq
    for g in range(N_KV):
        hg = g % 2
        parts = []
        for r in range(GROUP_R):
            h = g * GROUP_R + r
            slab = q_ref[0, :, (h // 2) * LANES:(h // 2 + 1) * LANES].astype(F32)
            if h % 2 != hg:
                slab = pltpu.roll(slab, HEAD_DIM, 1)
            parts.append(jnp.where(_lane_half_mask(slab.shape, hg), slab, 0.0))
        q4_ref[g] = jnp.concatenate(parts, axis=0).astype(q4_ref.dtype)


def _row_qpos(rows, cols, tq, q0):
    t = lax.broadcasted_iota(jnp.int32, (rows, cols), 0) & (tq - 1)
    return q0 + t


def _add_head_bias(s, bias, tq):
    rows, width = s.shape
    return (s.reshape(GROUP_R, tq, width) + bias[None]).reshape(rows, width)


def _softmax_attend(q4, k, v, bias, tq):
    t = _add_head_bias(_dot_nt(q4, k), bias, tq)
    m = jnp.max(t, axis=1, keepdims=True)
    e = jnp.exp2(t - m)
    l = jnp.sum(e, axis=1, keepdims=True)
    inv = jnp.where(m > 0.5 * NEG, 1.0 / l, 0.0)
    return _dot(e, v) * inv, e, inv


def _top_blocks(imp, qpos, n_blocks, axis):
    j = lax.broadcasted_iota(jnp.int32, imp.shape, axis)
    cur = qpos >> SEL_SHIFT
    forced = (j == 0) | (j == cur) | (j == cur - 1)
    valid = j * SEL_LEN <= qpos
    v = jnp.where(valid, imp + jnp.where(forced, SEL_FORCE, 0.0), -SEL_FORCE)
    v = jnp.where(j < n_blocks, v, -jnp.inf)
    sel = jnp.zeros(imp.shape, F32)
    jf = j.astype(F32)
    for _ in range(N_SEL):
        m = jnp.max(v, axis=axis, keepdims=True)
        first = jnp.min(jnp.where(v == m, jf, float(imp.shape[axis])), axis=axis, keepdims=True)
        hit = jf == first
        sel = jnp.where(hit, 1.0, sel)
        v = jnp.where(hit, -jnp.inf, v)
    return sel


def _compressed_and_select(q4_ref, kc_ref, vc_ref, ov_ref, oc_ref, selneg_ref, tq, q0, n_blocks,
                           blocks_on_rows):
    ncp = kc_ref.shape[1]
    n = lax.broadcasted_iota(jnp.int32, (tq, ncp), 1)
    bias_c = jnp.where(n * CMP_STRIDE + (CMP_LEN - 1) <= _row_qpos(tq, ncp, tq, q0), 0.0, NEG)
    psums = []
    for g in range(N_KV):
        sl = slice((g // 2) * LANES, (g // 2 + 1) * LANES)
        o, e, inv = _softmax_attend(q4_ref[g], kc_ref[0, :, sl], vc_ref[0, :, sl], bias_c, tq)
        oc_ref[g] = o
        p = e * inv
        ps = p[0:tq]
        for r in range(1, GROUP_R):
            ps = ps + p[r * tq:(r + 1) * tq]
        psums.append(ps)
    psum = jnp.concatenate(psums, axis=0)
    if blocks_on_rows:
        ov_t = ov_ref[...]
        imp_t = sum(_dot_nt(ov_t, piece) for piece in _split3(psum))
        qpos_t = q0 + (lax.broadcasted_iota(jnp.int32, imp_t.shape, 1) & (tq - 1))
        sel = _top_blocks(imp_t, qpos_t, n_blocks, 0).T
    else:
        imp = _dot_exact_rhs(psum, ov_ref[...])
        sel = _top_blocks(imp, _row_qpos(N_KV * tq, imp.shape[1], tq, q0), n_blocks, 1)
    selneg_ref[...] = jnp.where(sel > 0.5, 0.0, NEG).astype(selneg_ref.dtype)


def _selected_tile_update(q4_ref, k_of, v1_of, kv_feature_major, selneg, key_tile, first_block, key0,
                          tq, q0, m_ref, acc_ref):
    nb = selneg.shape[1]
    jj = lax.broadcasted_iota(jnp.int32, (nb, key_tile), 0)
    kk = lax.broadcasted_iota(jnp.int32, (nb, key_tile), 1)
    expand = jnp.where(jj == first_block + (kk >> SEL_SHIFT), 1.0, 0.0).astype(MXU_DT)
    bias_sel = jnp.dot(selneg, expand, preferred_element_type=F32)
    kpos = key0 + lax.broadcasted_iota(jnp.int32, (tq, key_tile), 1)
    causal = jnp.where(kpos <= _row_qpos(tq, key_tile, tq, q0), 0.0, NEG)
    for g in range(N_KV):
        bias = bias_sel[g * tq:(g + 1) * tq] + causal
        q4 = q4_ref[g]
        s = jnp.dot(q4, k_of(g), preferred_element_type=F32) if kv_feature_major else _dot_nt(q4, k_of(g))
        t = _add_head_bias(s, bias, tq)
        m_old = m_ref[g]
        m_new = jnp.maximum(m_old, jnp.max(t, axis=1, keepdims=True))
        alpha = jnp.exp2(m_old - m_new)
        e = jnp.concatenate([jnp.exp2(t[:, c * LANES:(c + 1) * LANES] - m_new)
                             for c in range(key_tile // LANES)], axis=1).astype(MXU_DT)
        pv = _dot_nt(e, v1_of(g)) if kv_feature_major else jnp.dot(e, v1_of(g), preferred_element_type=F32)
        acc_ref[g] = alpha * acc_ref[g] + pv
        m_ref[g] = m_new


def _init_softmax_state(m_ref, acc_ref):
    m_ref[...] = jnp.full(m_ref.shape, NEG, F32)
    acc_ref[...] = jnp.zeros(acc_ref.shape, F32)


def _gate_select_matrix():
    c = jnp.arange(LANES)[None, :, None]
    g = jnp.arange(N_KV)[:, None, None]
    blk = jnp.arange(GROUP_R * 3 * LANES)[None, None, :] // LANES
    sel = c == 3 * (GROUP_R * g + blk // 3) + blk % 3
    return jnp.concatenate([sel, sel], axis=1).astype(MXU_DT)


def _combine_and_store(gl_ref, gsel_ref, oc_ref, ow_ref, selected_of, o_ref, tq):
    gates = _sigmoid(gl_ref[0])
    g_hi = gates.astype(MXU_DT)
    g_lo = (gates - g_hi.astype(F32)).astype(MXU_DT)
    gate_pieces = jnp.concatenate([g_hi, g_lo], axis=1)
    for g in range(N_KV):
        hg = g % 2
        o_s = selected_of(g)
        o_c, o_w = oc_ref[g], ow_ref[g]
        lane_gates = jnp.dot(gate_pieces, gsel_ref[g], preferred_element_type=F32)
        heads = []
        for r in range(GROUP_R):
            h = g * GROUP_R + r
            rs = slice(r * tq, (r + 1) * tq)
            gc = [lane_gates[:, (3 * r + j) * LANES:(3 * r + j + 1) * LANES] for j in range(3)]
            x = gc[0] * o_c[rs] + gc[1] * o_s[rs] + gc[2] * o_w[rs]
            if h % 2 != hg:
                x = pltpu.roll(x, HEAD_DIM, 1)
            heads.append(x)
        for i in range(GROUP_R // 2):
            slab = jnp.where(_lane_half_mask(heads[0].shape, 0), heads[2 * i], heads[2 * i + 1])
            lo = (g * GROUP_R // 2 + i) * LANES
            o_ref[0, :, lo:lo + LANES] = slab


def _window_attend(q4_ref, kw, vw, bias, ow_ref, tq):
    for g in range(N_KV):
        sl = slice((g // 2) * LANES, (g // 2 + 1) * LANES)
        ow_ref[g] = _softmax_attend(q4_ref[g], kw[:, sl], vw[:, sl], bias, tq)[0]


def _attn_scratch(rows, tq, nbp):
    per_group = lambda dt: pltpu.VMEM((N_KV, rows, LANES), dt)
    return [per_group(MXU_DT), per_group(F32), per_group(F32), pltpu.VMEM((N_KV * tq, nbp), MXU_DT),
            per_group(F32), per_group(F32)]


def _ones_in_other_half(v, hg, axis):
    idx = lax.broadcasted_iota(jnp.int32, v.shape, axis)
    own = (idx >= HEAD_DIM) if hg else (idx < HEAD_DIM)
    return jnp.where(own, v, jnp.ones_like(v))


def _attn_prompt_kernel(*refs, tq, n_win):
    (q_ref, gl_ref, gsel_ref, kc_ref, vc_ref, ov_ref, ks_ref, vs_ref) = refs[:8]
    kw_refs = refs[8:8 + n_win]
    vw_refs = refs[8 + n_win:8 + 2 * n_win]
    o_ref = refs[8 + 2 * n_win]
    q4_ref, oc_ref, ow_ref, selneg_ref, m_ref, acc_ref = refs[9 + 2 * n_win:]
    i = pl.program_id(1)
    q0 = i * tq
    n_blocks = ks_ref.shape[1] // SEL_LEN

    _build_q4(q_ref, q4_ref, tq)
    _compressed_and_select(q4_ref, kc_ref, vc_ref, ov_ref, oc_ref, selneg_ref, tq, q0, n_blocks, True)

    kw = jnp.concatenate([r[0] for r in kw_refs], axis=0)
    vw = jnp.concatenate([r[0] for r in vw_refs], axis=0)
    lw = kw.shape[0]
    kpos = (i - (n_win - 1)) * tq + lax.broadcasted_iota(jnp.int32, (tq, lw), 1)
    qpos = _row_qpos(tq, lw, tq, q0)
    in_window = (kpos <= qpos) & (kpos >= qpos - WINDOW) & (kpos >= 0)
    _window_attend(q4_ref, kw, vw, jnp.where(in_window, 0.0, NEG), ow_ref, tq)

    _init_softmax_state(m_ref, acc_ref)
    selneg = selneg_ref[...]

    def body(kt, carry):
        k0 = pl.multiple_of(kt * KEY_TILE, KEY_TILE)
        slab = lambda g: slice((g // 2) * LANES, (g // 2 + 1) * LANES)
        k_of = lambda g: ks_ref[0, pl.ds(k0, KEY_TILE), slab(g)]
        v1_of = lambda g: _ones_in_other_half(vs_ref[0, pl.ds(k0, KEY_TILE), slab(g)], g % 2, 1)
        _selected_tile_update(q4_ref, k_of, v1_of, False, selneg, KEY_TILE, kt * BLOCKS_PER_TILE, k0,
                              tq, q0, m_ref, acc_ref)
        return carry

    lax.fori_loop(0, (q0 + tq + KEY_TILE - 1) // KEY_TILE, body, 0)

    def selected_of(g):
        acc = acc_ref[g]
        return acc / pltpu.roll(acc, HEAD_DIM, 1)

    _combine_and_store(gl_ref, gsel_ref, oc_ref, ow_ref, selected_of, o_ref, tq)


def _attn_prompt(q, glog, gsel, kc, vc, ov, ks, vs, kw, vw):
    b, t, d = q.shape
    tq = PROMPT_Q_TILE
    n_win = WINDOW // tq + 1
    nbp = ov.shape[0]
    blk = lambda w: pl.BlockSpec((1, tq, w), lambda bi, i: (bi, i, 0))
    per_b = lambda a: pl.BlockSpec((1,) + a.shape[1:], lambda bi, i: (bi, 0, 0))
    win = [pl.BlockSpec((1, tq, KV_WIDTH),
                        functools.partial(lambda bi, i, k: (bi, jnp.maximum(i - (n_win - 1) + k, 0), 0), k=k))
           for k in range(n_win)]
    rows = GROUP_R * tq
    return pl.pallas_call(
        functools.partial(_attn_prompt_kernel, tq=tq, n_win=n_win),
        grid=(b, t // tq),
        in_specs=[blk(d), blk(LANES), pl.BlockSpec(gsel.shape, lambda bi, i: (0, 0, 0)),
                  per_b(kc), per_b(vc), pl.BlockSpec(ov.shape, lambda bi, i: (0, 0)),
                  per_b(ks), per_b(vs)] + win + win,
        out_specs=blk(d),
        out_shape=jax.ShapeDtypeStruct((b, t, d), F32),
        scratch_shapes=_attn_scratch(rows, tq, nbp),
        compiler_params=_cparams(("arbitrary", "arbitrary")), name="attn_prompt",
    )(q, glog, gsel, kc, vc, ov, ks, vs, *([kw] * n_win), *([vw] * n_win))


def _attn_sample_kernel(pt_ref, q_ref, gl_ref, gsel_ref, kc_ref, vc_ref, ov_ref, kw_ref, vw_ref, kn_ref, vn_ref,
                        poolk_ref, poolv_ref, o_ref,
                        bufk, bufv, sems, q4_ref, oc_ref, ow_ref, selneg_ref, qbd_ref, m_ref, l_ref, acc_ref,
                        *, tq, past, n_chunks):
    b, c = pl.program_id(0), pl.program_id(1)
    q0 = past
    n_blocks = past // SEL_LEN + 1
    rows = GROUP_R * tq

    @pl.when(c < n_chunks)
    def _():
        _gather_pages_pipelined(pt_ref, b, c, n_chunks, (poolk_ref, poolv_ref), (bufk, bufv), sems)

    @pl.when(c == 0)
    def _():
        _build_q4(q_ref, q4_ref, tq)
        for g in range(N_KV):
            own, zero = q4_ref[g], jnp.zeros((rows, LANES), q4_ref.dtype)
            qbd_ref[g * rows:(g + 1) * rows, :] = jnp.concatenate([zero, own] if g // 2 else [own, zero], axis=1)
        _compressed_and_select(q4_ref, kc_ref, vc_ref, ov_ref, oc_ref, selneg_ref, tq, q0, n_blocks, False)
        lw = kw_ref.shape[1]
        kpos = past - WINDOW + lax.broadcasted_iota(jnp.int32, (tq, lw), 1)
        qpos = _row_qpos(tq, lw, tq, q0)
        in_window = (kpos <= qpos) & (kpos >= qpos - WINDOW)
        _window_attend(q4_ref, kw_ref[0], vw_ref[0], jnp.where(in_window, 0.0, NEG), ow_ref, tq)
        m_ref[...] = jnp.full(m_ref.shape, NEG, F32)
        l_ref[...] = jnp.zeros(l_ref.shape, F32)
        acc_ref[...] = jnp.zeros(acc_ref.shape, F32)

    def tile_update(k, v, feature_major, first_block, key0):
        kt = SAMPLE_KEY_TILE
        selneg = selneg_ref[...]
        nb = selneg.shape[1]
        jj = lax.broadcasted_iota(jnp.int32, (nb, kt), 0)
        kk = lax.broadcasted_iota(jnp.int32, (nb, kt), 1)
        expand = jnp.where(jj == first_block + (kk >> SEL_SHIFT), 1.0, 0.0).astype(MXU_DT)
        bias_sel = jnp.dot(selneg, expand, preferred_element_type=F32)
        kpos = key0 + lax.broadcasted_iota(jnp.int32, (tq, kt), 1)
        causal = jnp.where(kpos <= _row_qpos(tq, kt, tq, q0), 0.0, NEG)
        bias = bias_sel.reshape(N_KV, 1, tq, kt) + causal[None, None]
        qbd = qbd_ref[...]
        s = jnp.dot(qbd, k, preferred_element_type=F32) if feature_major else _dot_nt(qbd, k)
        t = (s.reshape(N_KV, GROUP_R, tq, kt) + bias).reshape(N_KV * rows, kt)
        m_old = m_ref[...]
        m_new = jnp.maximum(m_old, jnp.max(t, axis=1, keepdims=True))
        alpha = jnp.exp2(m_old - m_new)
        e = jnp.concatenate([jnp.exp2(t[:, i * LANES:(i + 1) * LANES] - m_new) for i in range(kt // LANES)],
                            axis=1)
        l_ref[...] = alpha * l_ref[...] + jnp.sum(e, axis=1, keepdims=True)
        e = e.astype(MXU_DT)
        pv = _dot_nt(e, v) if feature_major else jnp.dot(e, v, preferred_element_type=F32)
        acc_ref[...] = jnp.concatenate([alpha] * (KV_WIDTH // LANES), axis=1) * acc_ref[...] + pv
        m_ref[...] = m_new

    @pl.when(c < n_chunks)
    def _():
        pages_per_tile = SAMPLE_KEY_TILE // PAGE_SIZE
        slot = (b * n_chunks + c) % 2

        def body(kt, carry):
            p0 = kt * pages_per_tile
            k_t = jnp.concatenate([bufk[slot, p0 + j] for j in range(pages_per_tile)], axis=1).astype(MXU_DT)
            v_t = jnp.concatenate([bufv[slot, p0 + j] for j in range(pages_per_tile)], axis=1).astype(MXU_DT)
            page0 = c * PAGES_PER_STEP + p0
            tile_update(k_t, v_t, True, page0 * (PAGE_SIZE // SEL_LEN), page0 * PAGE_SIZE)
            return carry

        lax.fori_loop(0, PAGES_PER_STEP // pages_per_tile, body, 0)

    @pl.when(c == n_chunks)
    def _():
        tile_update(kn_ref[0], vn_ref[0], False, past // SEL_LEN, past)
        inv = 1.0 / l_ref[...]

        def selected_of(g):
            sl = slice((g // 2) * LANES, (g // 2 + 1) * LANES)
            return acc_ref[g * rows:(g + 1) * rows, sl] * inv[g * rows:(g + 1) * rows]

        _combine_and_store(gl_ref, gsel_ref, oc_ref, ow_ref, selected_of, o_ref, tq)


def _attn_sample(page_table, q, glog, gsel, kc, vc, ov, kw_full, vw_full, k_new, v_new, poolk, poolv):
    b, tq, d = q.shape
    n_pages = page_table.shape[1]
    past = n_pages * PAGE_SIZE
    n_chunks = n_pages // PAGES_PER_STEP
    rows = GROUP_R * tq
    nbp = ov.shape[1]
    per_b = lambda a: pl.BlockSpec((1,) + a.shape[1:], lambda bi, c, pt: (bi, 0, 0))
    any_spec = pl.BlockSpec(memory_space=pl.ANY)
    grid_spec = pltpu.PrefetchScalarGridSpec(
        num_scalar_prefetch=1, grid=(b, n_chunks + 1),
        in_specs=[per_b(q), per_b(glog), pl.BlockSpec(gsel.shape, lambda bi, c, pt: (0, 0, 0)),
                  per_b(kc), per_b(vc), pl.BlockSpec(ov.shape, lambda bi, c, pt: (0, 0)),
                  per_b(kw_full), per_b(vw_full), per_b(k_new), per_b(v_new), any_spec, any_spec],
        out_specs=per_b(q),
        scratch_shapes=[pltpu.VMEM((2, PAGES_PER_STEP, KV_WIDTH, PAGE_SIZE), F32),
                        pltpu.VMEM((2, PAGES_PER_STEP, KV_WIDTH, PAGE_SIZE), F32),
                        pltpu.SemaphoreType.DMA((2, 2))] + _attn_scratch(rows, tq, nbp)[:4]
                       + [pltpu.VMEM((N_KV * rows, KV_WIDTH), MXU_DT), pltpu.VMEM((N_KV * rows, LANES), F32),
                          pltpu.VMEM((N_KV * rows, LANES), F32), pltpu.VMEM((N_KV * rows, KV_WIDTH), F32)])
    return pl.pallas_call(
        functools.partial(_attn_sample_kernel, tq=tq, past=past, n_chunks=n_chunks),
        grid_spec=grid_spec,
        out_shape=jax.ShapeDtypeStruct((b, tq, d), F32),
        compiler_params=_cparams(("arbitrary", "arbitrary")), name="attn_sample",
    )(page_table, q, glog, gsel, kc, vc, ov, kw_full, vw_full, k_new, v_new, poolk, poolv)


def _win_update_kernel(ck_ref, cv_ref, nk_ref, nv_ref, ok_ref, ov_ref, fk_ref, fv_ref):
    t = nk_ref.shape[1]
    wb = ck_ref.shape[1]
    for c_ref, n_ref, o_ref, f_ref in ((ck_ref, nk_ref, ok_ref, fk_ref), (cv_ref, nv_ref, ov_ref, fv_ref)):
        o_ref[0, 0:wb - t, :] = c_ref[0, t:wb, :]
        o_ref[0, wb - t:wb, :] = n_ref[0]
        pad = jnp.zeros((f_ref.shape[1] - wb - t, f_ref.shape[2]), F32)
        f_ref[0] = jnp.concatenate([c_ref[0], n_ref[0], pad], axis=0).astype(f_ref.dtype)


def _win_update(cache_k, cache_v, new_k, new_v):
    b, wb, w = cache_k.shape
    t = new_k.shape[1]
    full_len = wb + LANES
    cs = pl.BlockSpec((1, wb, w), lambda i: (i, 0, 0))
    ns = pl.BlockSpec((1, t, w), lambda i: (i, 0, 0))
    fs = pl.BlockSpec((1, full_len, w), lambda i: (i, 0, 0))
    return pl.pallas_call(
        _win_update_kernel, grid=(b,), in_specs=[cs, cs, ns, ns], out_specs=[cs, cs, fs, fs],
        out_shape=[jax.ShapeDtypeStruct((b, wb, w), F32)] * 2
                  + [jax.ShapeDtypeStruct((b, full_len, w), MXU_DT)] * 2,
        compiler_params=_cparams(("arbitrary",)), name="win_update",
    )(cache_k, cache_v, new_k, new_v)


def _overlap_matrix(n_cmp_pad, n_cmp, n_blocks_pad, n_blocks):
    n = jnp.arange(n_cmp_pad)[:, None]
    j = jnp.arange(n_blocks_pad)[None, :]
    c_start, s_start = n * CMP_STRIDE, j * SEL_LEN
    ov = (c_start < s_start + SEL_LEN) & (c_start + CMP_LEN > s_start) & (n < n_cmp) & (j < n_blocks)
    return ov.astype(MXU_DT)


def _finish_rows(a, w_ref, gpost_ref, x_ref, gate_ref, y_ref):
    o = jnp.dot(a.astype(MXU_DT), w_ref[...], preferred_element_type=F32)
    n = o * lax.rsqrt(jnp.mean(o * o, axis=-1, keepdims=True) + RMS_EPS) * gpost_ref[...]
    y_ref[0] = x_ref[0] + gate_ref[0] * n


def _attn_out_kernel(o_ref, z_ref, w_ref, gpost_ref, x_ref, gate_ref, y_ref):
    _finish_rows(o_ref[0] * _silu(z_ref[0]), w_ref, gpost_ref, x_ref, gate_ref, y_ref)


def _ssm_out_kernel(y_ref, z_ref, wglu_ref, w_ref, gpost_ref, x_ref, gate_ref, o_ref):
    v = jnp.dot(y_ref[0].astype(MXU_DT), wglu_ref[...], preferred_element_type=F32)
    half = v.shape[1] // 2
    glu = v[:, :half] * _sigmoid(v[:, half:])
    _finish_rows(glu * _silu(z_ref[0]), w_ref, gpost_ref, x_ref, gate_ref, o_ref)


def _mixer_out(body, name, acts, weights, g_post, x, gate):
    b, t, d = x.shape
    tm = min(t, 256)
    row = pl.BlockSpec((1, tm, d), lambda i, j: (i, j, 0))
    full = lambda a: pl.BlockSpec(a.shape, lambda i, j: (0,) * a.ndim)
    per_b = pl.BlockSpec((1, 1, d), lambda i, j: (i, 0, 0))
    return pl.pallas_call(
        body, grid=(b, t // tm),
        in_specs=[row] * len(acts) + [full(w) for w in weights] + [full(g_post), row, per_b],
        out_specs=row, out_shape=jax.ShapeDtypeStruct((b, t, d), F32),
        compiler_params=_cparams(("arbitrary", "arbitrary")), name=name,
    )(*acts, *weights, g_post, x, gate)


def _ssm_proj_kernel(x_ref, shift_ref, scale_ref, g_ref, w_ref, u_ref, z_ref):
    h = _rms_mod(x_ref[0], g_ref[...], scale_ref[0], shift_ref[0]).astype(MXU_DT)
    p = jnp.dot(h, w_ref[...], preferred_element_type=F32)
    half = p.shape[1] // 2
    u_ref[0] = p[:, :half]
    z_ref[0] = p[:, half:]


def _ssm_proj(x, shift, scale, g_pre, w):
    b, t, d = x.shape
    tm = min(t, 256)
    row = pl.BlockSpec((1, tm, d), lambda i, j: (i, j, 0))
    per_b = pl.BlockSpec((1, 1, d), lambda i, j: (i, 0, 0))
    full = lambda a: pl.BlockSpec(a.shape, lambda i, j: (0,) * a.ndim)
    out = jax.ShapeDtypeStruct((b, t, d), F32)
    return pl.pallas_call(
        _ssm_proj_kernel, grid=(b, t // tm),
        in_specs=[row, per_b, per_b, full(g_pre), full(w)], out_specs=[row, row], out_shape=[out, out],
        compiler_params=_cparams(("arbitrary", "arbitrary")), name="ssm_proj",
    )(x, shift, scale, g_pre, w)


def _s5_prep_kernel(ar_ref, ai_ref, ldt_ref, bre_ref, bim_ref, lre_ref, lim_ref, cbre_ref, cbim_ref):
    ar, ai = ar_ref[...], ai_ref[...]
    dt = jnp.exp(ldt_ref[...])
    mag = jnp.exp(dt * ar)
    lam_re, lam_im = mag * jnp.cos(dt * ai), mag * jnp.sin(dt * ai)
    den = ar * ar + ai * ai
    nr, ni = lam_re - 1.0, lam_im
    coef_re, coef_im = (nr * ar + ni * ai) / den, (ni * ar - nr * ai) / den
    lre_ref[...] = lam_re
    lim_ref[...] = lam_im
    cr, ci = coef_re[:, None, :], coef_im[:, None, :]
    b_re, b_im = bre_ref[...], bim_ref[...]
    cbre_ref[...] = cr * b_re - ci * b_im
    cbim_ref[...] = cr * b_im + ci * b_re


def _s5_prep(a_re, a_im, log_dt, b_re_t, b_im_t):
    ng, p = a_re.shape
    full = lambda a: pl.BlockSpec(a.shape, lambda: (0,) * a.ndim)
    args = (a_re, a_im, log_dt.reshape(ng, 1), b_re_t, b_im_t)
    lam = jax.ShapeDtypeStruct((ng, p), F32)
    cb = jax.ShapeDtypeStruct(b_re_t.shape, F32)
    return pl.pallas_call(
        _s5_prep_kernel, in_specs=[full(a) for a in args],
        out_specs=[full(lam), full(lam), full(cb), full(cb)], out_shape=[lam, lam, cb, cb],
        name="s5_prep",
    )(*args)


SCAN_LANES = 512


def _s5_scan_kernel(u_ref, d_ref, bre_ref, bim_ref, cre_ref, cim_ref, lre_ref, lim_ref, h0re_ref, h0im_ref,
                    y_ref, hre_ref, him_ref, sre, sim, *, n_batch, nbp, tt):
    ti = pl.program_id(1)
    n_lc = sre.shape[0]

    @pl.when(ti == 0)
    def _():
        hre_ref[...] = h0re_ref[...]
        him_ref[...] = h0im_ref[...]
        if nbp != n_batch:
            sre[...] = jnp.zeros(sre.shape, F32)
            sim[...] = jnp.zeros(sim.shape, F32)

    def drive(b, carry):
        ub = u_ref[b].astype(MXU_DT)
        for s_ref, w_ref in ((sre, bre_ref), (sim, bim_ref)):
            bb = jnp.dot(ub, w_ref[0], preferred_element_type=F32)
            for c in range(n_lc):
                s_ref[c, pl.ds(b, tt, stride=nbp), :] = bb[:, c * LANES:(c + 1) * LANES]
        return carry

    lax.fori_loop(0, n_batch, drive, 0)

    per_scan = SCAN_LANES // LANES
    for sb in range(nbp // SUBLANES):
        for lc in range(n_lc // per_scan):
            chunks = range(lc * per_scan, (lc + 1) * per_scan)
            srow = slice(sb * SUBLANES, (sb + 1) * SUBLANES)
            lane = lambda c: slice(c * LANES, (c + 1) * LANES)
            lam_re = [jnp.broadcast_to(lre_ref[:, lane(c)], (SUBLANES, LANES)) for c in chunks]
            lam_im = [jnp.broadcast_to(lim_ref[:, lane(c)], (SUBLANES, LANES)) for c in chunks]

            def step(t, carry):
                r0 = pl.multiple_of(t * nbp + sb * SUBLANES, SUBLANES)
                new = []
                for k, c in enumerate(chunks):
                    h_re, h_im = carry[k]
                    n_re = lam_re[k] * h_re - lam_im[k] * h_im + sre[c, pl.ds(r0, SUBLANES), :]
                    n_im = lam_re[k] * h_im + lam_im[k] * h_re + sim[c, pl.ds(r0, SUBLANES), :]
                    sre[c, pl.ds(r0, SUBLANES), :] = n_re
                    sim[c, pl.ds(r0, SUBLANES), :] = n_im
                    new.append((n_re, n_im))
                return tuple(new)

            init = tuple((hre_ref[srow, lane(c)], him_ref[srow, lane(c)]) for c in chunks)
            final = lax.fori_loop(0, tt, step, init, unroll=min(tt, 8))
            for k, c in enumerate(chunks):
                hre_ref[srow, lane(c)] = final[k][0]
                him_ref[srow, lane(c)] = final[k][1]

    def readout(b, carry):
        rows = lambda s_ref: jnp.concatenate(
            [s_ref[c, pl.ds(b, tt, stride=nbp), :] for c in range(n_lc)], axis=1).astype(MXU_DT)
        y = (jnp.dot(rows(sre), cre_ref[0], preferred_element_type=F32)
             - jnp.dot(rows(sim), cim_ref[0], preferred_element_type=F32))
        y_ref[b] = y + d_ref[...] * u_ref[b]
        return carry

    lax.fori_loop(0, n_batch, readout, 0)


def _s5_scan(u, d_skip, b_re_bd, b_im_bd, c_re_bd, c_im_bd, lam_re, lam_im, h0_re, h0_im, nbp):
    b, t, w = u.shape
    q, wq, sq = b_re_bd.shape
    ns = q * sq
    tt = min(t, 256)
    ublk = pl.BlockSpec((b, tt, wq), lambda k, i: (0, i, k))
    dblk = pl.BlockSpec((1, wq), lambda k, i: (0, k))
    bblk = pl.BlockSpec((1, wq, sq), lambda k, i: (k, 0, 0))
    cblk = pl.BlockSpec((1, sq, wq), lambda k, i: (k, 0, 0))
    lblk = pl.BlockSpec((1, sq), lambda k, i: (0, k))
    hblk = pl.BlockSpec((nbp, sq), lambda k, i: (0, k))
    st = jax.ShapeDtypeStruct((nbp, ns), F32)
    return pl.pallas_call(
        functools.partial(_s5_scan_kernel, n_batch=b, nbp=nbp, tt=tt),
        grid=(q, t // tt),
        in_specs=[ublk, dblk, bblk, bblk, cblk, cblk, lblk, lblk, hblk, hblk],
        out_specs=[ublk, hblk, hblk],
        out_shape=[jax.ShapeDtypeStruct((b, t, w), F32), st, st],
        scratch_shapes=[pltpu.VMEM((sq // LANES, tt * nbp, LANES), F32)] * 2,
        compiler_params=_cparams(("arbitrary", "arbitrary")), name="s5_scan",
    )(u, d_skip, b_re_bd, b_im_bd, c_re_bd, c_im_bd, lam_re, lam_im, h0_re, h0_im)


def _s5_block_diag(cb_re, cb_im, c_re, c_im):
    ng, c, p = cb_re.shape
    gq = ng // S5_QUARTERS
    eye = jnp.eye(gq, dtype=F32)

    def drive(w):
        w4 = w.reshape(S5_QUARTERS, gq, c, p)
        return jnp.einsum('qgcp,gk->qgckp', w4, eye).reshape(S5_QUARTERS, gq * c, gq * p).astype(MXU_DT)

    def readout(w):
        w4 = w.reshape(S5_QUARTERS, gq, c, p)
        return jnp.einsum('qgcp,gk->qgpkc', w4, eye).reshape(S5_QUARTERS, gq * p, gq * c).astype(MXU_DT)

    return drive(cb_re), drive(cb_im), readout(c_re), readout(c_im)


def _nsa_layer(xp, xs, modp, mods, caches, page_table, g_pre, g_post, w_in, pe, wk1, wk2, wv1, wv2, w_out):
    cache_kc, cache_vc, cache_ks, cache_vs, cache_kw, cache_vw = caches
    b, t, d = xp.shape
    sb, st, _ = xs.shape
    n_pages = page_table.shape[1]
    past = n_pages * PAGE_SIZE
    assert st < CMP_STRIDE and n_pages % PAGES_PER_STEP == 0 and t % KEY_TILE == 0

    aw = N_HEADS * HEAD_DIM
    cuts = [aw, aw + 6 * KV_WIDTH, aw + 6 * KV_WIDTH + 3 * N_HEADS]
    wq = w_in[:, :cuts[0]].astype(MXU_DT)
    wkv = w_in[:, cuts[0]:cuts[1]].astype(MXU_DT)
    wg = jnp.pad(w_in[:, cuts[1]:cuts[2]], ((0, 0), (0, LANES - 3 * N_HEADS))).astype(MXU_DT)
    wz = w_in[:, cuts[2]:].astype(MXU_DT)
    g_pre2, g_post2 = g_pre.reshape(1, d), g_post.reshape(1, d)

    proj_p = _attn_proj(xp, modp[0], modp[1], g_pre2, _rope_tables(jnp.arange(t)), wq, wkv, wg, wz)
    proj_s = _attn_proj(xs, mods[0], mods[1], g_pre2, _rope_tables(past + jnp.arange(st)), wq, wkv, wg, wz)
    (q_p, kc_p, vc_p, ks_p, vs_p, kw_p, vw_p, ksb_p, vsb_p, kwb_p, vwb_p, gl_p, z_p) = proj_p
    (q_s, kc_s, vc_s, ks_s, vs_s, kw_s, vw_s, ksb_s, vsb_s, _, _, gl_s, z_s) = proj_s

    wk_big, w1k_t, w2k_bd = _cmp_weights(wk1, wk2)
    wv_big, w1v_t, w2v_bd = _cmp_weights(wv1, wv2)
    pe_col = pe.reshape(CMP_LEN * HEAD_DIM, 1)
    chunk_w = CMP_STRIDE * KV_WIDTH

    n_ch = t // CMP_STRIDE
    pk, pv = _cmp_part(kc_p.reshape(b * n_ch, chunk_w), vc_p.reshape(b * n_ch, chunk_w), wk_big, wv_big)
    kcc_p, vcc_p = _cmp_finish(pk.reshape(b, n_ch, -1), pv.reshape(b, n_ch, -1), pe_col,
                               w1k_t, w1v_t, w2k_bd, w2v_bd)
    ns_p = -(-t // SEL_LEN)
    nbp_p = -(-ns_p // LANES) * LANES
    ov_p = _overlap_matrix(n_ch, n_ch - CMP_RATIO + 1, nbp_p, ns_p).T
    gsel = _gate_select_matrix()
    o_p = _attn_prompt(q_p, gl_p, gsel, kcc_p, vcc_p, ov_p, ksb_p, vsb_p, kwb_p, vwb_p)
    y_p = _mixer_out(_attn_out_kernel, "attn_out", (o_p, z_p), (w_out.astype(MXU_DT),), g_post2, xp, modp[2])

    cpp = PAGE_SIZE // CMP_STRIDE
    n_phys = cache_kc.shape[0]
    feature_major = lambda pool: pool.transpose(0, 2, 3, 1).reshape(n_phys, KV_WIDTH, PAGE_SIZE)
    pk, pv = _cmp_part_paged(page_table, feature_major(cache_kc), feature_major(cache_vc), wk_big, wv_big)
    kcc_s, vcc_s = _cmp_finish(pk, pv, pe_col, w1k_t, w1v_t, w2k_bd, w2v_bd)
    n_ch_s = (past + st) // CMP_STRIDE
    ns_s = -(-(past + st) // SEL_LEN)
    nbp_s = -(-ns_s // LANES) * LANES
    ov_s = _overlap_matrix(n_pages * cpp, n_ch_s - CMP_RATIO + 1, nbp_s, ns_s)
    wb = cache_kw.shape[1]
    win_k, win_v, kw_full, vw_full = _win_update(cache_kw.reshape(sb, wb, KV_WIDTH),
                                                 cache_vw.reshape(sb, wb, KV_WIDTH), kw_s, vw_s)
    pad_new = lambda a: jnp.pad(a, ((0, 0), (0, SAMPLE_KEY_TILE - st), (0, 0)))
    o_s = _attn_sample(page_table, q_s, gl_s, gsel, kcc_s, vcc_s, ov_s, kw_full, vw_full,
                       pad_new(ksb_s), pad_new(vsb_s), feature_major(cache_ks), feature_major(cache_vs))
    y_s = _mixer_out(_attn_out_kernel, "attn_out_s", (o_s, z_s), (w_out.astype(MXU_DT),), g_post2, xs, mods[2])

    heads = lambda a: a.reshape(a.shape[0], a.shape[1], N_KV, HEAD_DIM)
    wl = min(WINDOW, t)
    rows_p = (kc_p, vc_p, ks_p, vs_p, kw_p[:, t - wl:], vw_p[:, t - wl:])
    rows_s = (kc_s, vc_s, ks_s, vs_s, win_k, win_v)
    return y_p, y_s, [heads(a) for a in rows_p], [heads(a) for a in rows_s]


def _s5_layer(xp, xs, modp, mods, state_re, state_im, g_pre, g_post, w_in, a_re, a_im, log_dt,
              b_re, b_im, c_re, c_im, d_skip, w_glu, w_out):
    b, t, d = xp.shape
    sb, st, _ = xs.shape
    ng, p, c = b_re.shape
    g_pre2, g_post2 = g_pre.reshape(1, d), g_post.reshape(1, d)
    w_in16, w_glu16, w_out16 = w_in.astype(MXU_DT), w_glu.astype(MXU_DT), w_out.astype(MXU_DT)
    lam_re, lam_im, cb_re, cb_im = _s5_prep(a_re, a_im, log_dt, b_re.transpose(0, 2, 1), b_im.transpose(0, 2, 1))
    bd = _s5_block_diag(cb_re, cb_im, c_re, c_im)
    lam = (lam_re.reshape(1, ng * p), lam_im.reshape(1, ng * p))
    d2 = d_skip.reshape(1, d)

    outs = []
    for x, mod, h0 in ((xp, modp, None), (xs, mods, (state_re, state_im))):
        nb = x.shape[0]
        nbp = -(-nb // SUBLANES) * SUBLANES
        u, z = _ssm_proj(x, mod[0], mod[1], g_pre2, w_in16)
        if h0 is None:
            h0_re = h0_im = jnp.zeros((nbp, ng * p), F32)
        else:
            h0_re, h0_im = (jnp.pad(h.reshape(nb, ng * p), ((0, nbp - nb), (0, 0))) for h in h0)
        y, h_re, h_im = _s5_scan(u, d2, *bd, *lam, h0_re, h0_im, nbp)
        out = _mixer_out(_ssm_out_kernel, "ssm_out", (y, z), (w_glu16, w_out16), g_post2, x, mod[2])
        outs.append((out, h_re[:nb].reshape(nb, ng, p), h_im[:nb].reshape(nb, ng, p)))
    return outs


def kernel(x_prompt, x_sample, cache_k_cmp, cache_v_cmp, cache_k_sel, cache_v_sel, cache_k_win, cache_v_win, state_s5_re, state_s5_im, page_table, c_prompt, c_sample, norm_pre, norm_post, w_ada, b_ada, w_in_attn, pe_cmp, w_cmp_k1, w_cmp_k2, w_cmp_v1, w_cmp_v2, w_out_attn, w_in_ssm, s5_a_re, s5_a_im, s5_log_dt, s5_b_re, s5_b_im, s5_c_re, s5_c_im, s5_d, w_glu, w_out_ssm):
    b, t, d = x_prompt.shape
    sb = x_sample.shape[0]
    depth = w_ada.shape[0]
    c_all = jnp.concatenate([c_prompt, c_sample], axis=0)
    n_c = -(-c_all.shape[0] // SUBLANES) * SUBLANES
    mod = _ada(jnp.pad(c_all, ((0, n_c - c_all.shape[0]), (0, 0))), w_ada, b_ada)

    yp, ys = x_prompt, x_sample
    attn_p, attn_s, ssm_p, ssm_s = [], [], [], []
    for i in range(depth):
        parts = [mod[i, :, k * d:(k + 1) * d] for k in range(3)]
        modp = [a[:b].reshape(b, 1, d) for a in parts]
        mods = [a[b:b + sb].reshape(sb, 1, d) for a in parts]
        li = i // 2
        if i % 2 == 0:
            caches = (cache_k_cmp[li], cache_v_cmp[li], cache_k_sel[li], cache_v_sel[li],
                      cache_k_win[li], cache_v_win[li])
            yp, ys, rows_p, rows_s = _nsa_layer(
                yp, ys, modp, mods, caches, page_table, norm_pre[i], norm_post[i], w_in_attn[li], pe_cmp[li],
                w_cmp_k1[li], w_cmp_k2[li], w_cmp_v1[li], w_cmp_v2[li], w_out_attn[li])
            attn_p.append(rows_p)
            attn_s.append(rows_s)
        else:
            (yp, pr, pi), (ys, sr, si) = _s5_layer(
                yp, ys, modp, mods, state_s5_re[li], state_s5_im[li], norm_pre[i], norm_post[i], w_in_ssm[li],
                s5_a_re[li], s5_a_im[li], s5_log_dt[li], s5_b_re[li], s5_b_im[li], s5_c_re[li], s5_c_im[li],
                s5_d[li], w_glu[li], w_out_ssm[li])
            ssm_p.append((pr, pi))
            ssm_s.append((sr, si))

    outs = [yp, ys]
    for n in range(6):
        outs.append(jnp.stack([rows[n] for rows in attn_p]))
        outs.append(jnp.stack([rows[n] for rows in attn_s]))
    for n in range(2):
        outs.append(jnp.stack([st[n] for st in ssm_p]))
        outs.append(jnp.stack([st[n] for st in ssm_s]))
    return tuple(outs)
```

```python
import functools
import math

import jax
import jax.numpy as jnp
from jax import lax
from jax.experimental import pallas as pl
from jax.experimental.pallas import tpu as pltpu

N_HEADS = 16
HEAD_DIM = 64
N_KV = 4
GROUP_R = N_HEADS // N_KV
ROT_HALF = HEAD_DIM // 8
ROPE_THETA = 500000.0
CMP_LEN = 32
CMP_STRIDE = 16
CMP_RATIO = CMP_LEN // CMP_STRIDE
CMP_HIDDEN = 64
SEL_LEN = 64
SEL_SHIFT = SEL_LEN.bit_length() - 1
N_SEL = 16
WINDOW = 512
PAGE_SIZE = 128
S5_GROUP = 16
S5_STATE = 64
NEG = -1e30
LOG2_E = math.log2(math.e)
SEL_FORCE = 1e4
RMS_EPS = 1e-6
KV_WIDTH = N_KV * HEAD_DIM

LANES = 128
SUBLANES = 8
VMEM_LIMIT_BYTES = 56 * 1024 * 1024

MXU_DT = jnp.bfloat16
F32 = jnp.float32

PROMPT_Q_TILE = 256
KEY_TILE = 256
BLOCKS_PER_TILE = KEY_TILE // SEL_LEN
SAMPLE_KEY_TILE = 1024
PAGES_PER_STEP = 32
S5_QUARTERS = 4


def _cparams(sem):
    return pltpu.CompilerParams(dimension_semantics=sem, vmem_limit_bytes=VMEM_LIMIT_BYTES)


def _dot(a, b):
    return jnp.dot(a.astype(MXU_DT), b.astype(MXU_DT), preferred_element_type=F32)


def _dot_nt(a, b):
    return lax.dot_general(a.astype(MXU_DT), b.astype(MXU_DT), (((1,), (1,)), ((), ())),
                           preferred_element_type=F32)


def _split3(x):
    hi = x.astype(MXU_DT)
    r1 = x - hi.astype(F32)
    mid = r1.astype(MXU_DT)
    lo = (r1 - mid.astype(F32)).astype(MXU_DT)
    return hi, mid, lo


def _dot_exact_rhs(x, m):
    hi, mid, lo = _split3(x)
    return (jnp.dot(hi, m, preferred_element_type=F32) + jnp.dot(mid, m, preferred_element_type=F32)
            + jnp.dot(lo, m, preferred_element_type=F32))


def _dot_f32(a, b):
    ah, am, _ = _split3(a)
    bh, bm, _ = _split3(b)
    d = lambda p, q: jnp.dot(p, q, preferred_element_type=F32)
    return d(ah, bh) + (d(ah, bm) + d(am, bh)) + d(am, bm)


def _sigmoid(x):
    return 1.0 / (1.0 + jnp.exp(-x))


def _silu(x):
    return x * _sigmoid(x)


def _ada_kernel(c_ref, w_ref, b_ref, o_ref):
    o_ref[...] = _dot_f32(c_ref[...], w_ref[...]) + b_ref[...]


def _ada(c_all, w_ada, b_ada):
    depth, d, d3 = w_ada.shape
    n = c_all.shape[0]
    return pl.pallas_call(
        _ada_kernel,
        grid=(depth, d3 // d),
        in_specs=[pl.BlockSpec((n, d), lambda i, j: (0, 0)),
                  pl.BlockSpec((None, d, d), lambda i, j: (i, 0, j)),
                  pl.BlockSpec((None, 1, d), lambda i, j: (i, 0, j))],
        out_specs=pl.BlockSpec((None, n, d), lambda i, j: (i, 0, j)),
        out_shape=jax.ShapeDtypeStruct((depth, n, d3), F32),
        compiler_params=_cparams(("arbitrary", "arbitrary")),
        name="ada_mod",
    )(c_all, w_ada, b_ada.reshape(depth, 1, d3))


def _rms_mod(x, g, scale, shift):
    y = x * lax.rsqrt(jnp.mean(x * x, axis=-1, keepdims=True) + RMS_EPS)
    return (y * g) * (1.0 + scale) + shift


def _rope_slab(x, c, s1, s2):
    return x * c + pltpu.roll(x, ROT_HALF, 1) * s1 + pltpu.roll(x, LANES - ROT_HALF, 1) * s2


def _attn_proj_kernel(x_ref, shift_ref, scale_ref, g_ref, c_ref, s1_ref, s2_ref,
                      wq_ref, wkv_ref, wg_ref, wz_ref,
                      q_ref, kc_ref, vc_ref, ks_ref, vs_ref, kw_ref, vw_ref,
                      ksb_ref, vsb_ref, kwb_ref, vwb_ref, gl_ref, z_ref):
    h = _rms_mod(x_ref[0], g_ref[...], scale_ref[0], shift_ref[0]).astype(MXU_DT)
    c, s1, s2 = c_ref[...], s1_ref[...], s2_ref[...]
    q = jnp.dot(h, wq_ref[...], preferred_element_type=F32)
    qscale = HEAD_DIM ** -0.5 * LOG2_E
    for j in range(q.shape[1] // LANES):
        sl = slice(j * LANES, (j + 1) * LANES)
        q_ref[0, :, sl] = (_rope_slab(q[:, sl], c, s1, s2) * qscale).astype(q_ref.dtype)
    kv = jnp.dot(h, wkv_ref[...], preferred_element_type=F32)
    outs = ((kc_ref, None, False), (vc_ref, None, False), (ks_ref, ksb_ref, True),
            (vs_ref, vsb_ref, False), (kw_ref, kwb_ref, True), (vw_ref, vwb_ref, False))
    for n, (o_ref, ob_ref, roped) in enumerate(outs):
        for j in range(KV_WIDTH // LANES):
            lo = n * KV_WIDTH + j * LANES
            v = kv[:, lo:lo + LANES]
            if roped:
                v = _rope_slab(v, c, s1, s2)
            o_ref[0, :, j * LANES:(j + 1) * LANES] = v
            if ob_ref is not None:
                ob_ref[0, :, j * LANES:(j + 1) * LANES] = v.astype(ob_ref.dtype)
    gl_ref[0] = jnp.dot(h, wg_ref[...], preferred_element_type=F32)
    z_ref[0] = jnp.dot(h, wz_ref[...], preferred_element_type=F32)


def _attn_proj(x, shift, scale, g_pre, tables, wq, wkv, wg, wz):
    b, t, d = x.shape
    tm = min(t, 256)
    cos_t, s1_t, s2_t = tables
    row = lambda i, j: (i, j, 0)
    per_b = pl.BlockSpec((1, 1, d), lambda i, j: (i, 0, 0))
    full = lambda a: pl.BlockSpec(a.shape, lambda i, j: (0,) * a.ndim)
    tab = pl.BlockSpec((tm, LANES), lambda i, j: (j, 0))
    kv32 = jax.ShapeDtypeStruct((b, t, KV_WIDTH), F32)
    kv16 = jax.ShapeDtypeStruct((b, t, KV_WIDTH), MXU_DT)
    kvspec = pl.BlockSpec((1, tm, KV_WIDTH), row)
    return pl.pallas_call(
        _attn_proj_kernel,
        grid=(b, t // tm),
        in_specs=[pl.BlockSpec((1, tm, d), row), per_b, per_b, full(g_pre), tab, tab, tab,
                  full(wq), full(wkv), full(wg), full(wz)],
        out_specs=[pl.BlockSpec((1, tm, d), row)] + [kvspec] * 10
                  + [pl.BlockSpec((1, tm, LANES), row), pl.BlockSpec((1, tm, d), row)],
        out_shape=[jax.ShapeDtypeStruct((b, t, d), MXU_DT)] + [kv32] * 6 + [kv16] * 4
                  + [jax.ShapeDtypeStruct((b, t, LANES), F32), jax.ShapeDtypeStruct((b, t, d), F32)],
        compiler_params=_cparams(("arbitrary", "arbitrary")),
        name="attn_proj",
    )(x, shift, scale, g_pre, cos_t, s1_t, s2_t, wq, wkv, wg, wz)


def _rope_tables(pos):
    inv = ROPE_THETA ** (-jnp.arange(ROT_HALF, dtype=F32) / ROT_HALF)
    ang = pos.astype(F32)[:, None] * inv[None]
    cos, sin = jnp.cos(ang), jnp.sin(ang)
    n = pos.shape[0]
    rest = HEAD_DIM - 2 * ROT_HALF
    zeros = lambda w: jnp.zeros((n, w), F32)
    c64 = jnp.concatenate([cos, cos, jnp.ones((n, rest), F32)], axis=1)
    s1 = jnp.concatenate([zeros(ROT_HALF), sin, zeros(rest)], axis=1)
    s2 = jnp.concatenate([-sin, zeros(ROT_HALF + rest)], axis=1)
    rep = LANES // HEAD_DIM
    return tuple(jnp.tile(a, (1, rep)) for a in (c64, s1, s2))


def _cmp_part_kernel(xk_ref, xv_ref, wk_ref, wv_ref, pk_ref, pv_ref):
    pk_ref[...] = _dot(xk_ref[...], wk_ref[...])
    pv_ref[...] = _dot(xv_ref[...], wv_ref[...])


def _cmp_part(xk, xv, wk, wv):
    rows, width = xk.shape
    tm = min(rows, 256)
    n = wk.shape[1]
    xs = pl.BlockSpec((tm, width), lambda i: (i, 0))
    ws = pl.BlockSpec(wk.shape, lambda i: (0, 0))
    os_ = pl.BlockSpec((tm, n), lambda i: (i, 0))
    return pl.pallas_call(
        _cmp_part_kernel, grid=(rows // tm,), in_specs=[xs, xs, ws, ws], out_specs=[os_, os_],
        out_shape=[jax.ShapeDtypeStruct((rows, n), F32)] * 2,
        compiler_params=_cparams(("arbitrary",)), name="cmp_part",
    )(xk, xv, wk, wv)


def _pages_dma(pt_ref, b, chunk, pools, bufs, sems, slot, wait):
    def body(p, carry):
        page = pt_ref[b, chunk * PAGES_PER_STEP + p]
        for k, (pool, buf) in enumerate(zip(pools, bufs)):
            copy = pltpu.make_async_copy(pool.at[page], buf.at[slot, p], sems.at[slot, k])
            if wait:
                copy.wait()
            else:
                copy.start()
        return carry

    lax.fori_loop(0, PAGES_PER_STEP, body, 0)


def _gather_pages_pipelined(pt_ref, b, c, n_chunks, pools, bufs, sems):
    step = b * n_chunks + c
    slot = step % 2
    more_in_batch = c + 1 < n_chunks
    next_b = jnp.where(more_in_batch, b, b + 1)
    next_c = jnp.where(more_in_batch, c + 1, 0)

    @pl.when(step == 0)
    def _():
        _pages_dma(pt_ref, b, c, pools, bufs, sems, slot, wait=False)

    @pl.when(more_in_batch | (b + 1 < pl.num_programs(0)))
    def _():
        _pages_dma(pt_ref, next_b, next_c, pools, bufs, sems, 1 - slot, wait=False)

    _pages_dma(pt_ref, b, c, pools, bufs, sems, slot, wait=True)
    return slot


def _cmp_part_paged_kernel(pt_ref, poolk_ref, poolv_ref, perm_ref, wk_ref, wv_ref, pk_ref, pv_ref,
                           bufk, bufv, xk, xv, sems):
    b, c = pl.program_id(0), pl.program_id(1)
    slot = _gather_pages_pipelined(pt_ref, b, c, pl.num_programs(1), (poolk_ref, poolv_ref),
                                   (bufk, bufv), sems)
    cpp = PAGE_SIZE // CMP_STRIDE
    perm = perm_ref[...]

    def page_rows(p, carry):
        r0 = pl.multiple_of(p * cpp, cpp)
        for buf, x in ((bufk, xk), (bufv, xv)):
            y = _dot_nt(perm, buf[slot, p])
            for s in range(CMP_STRIDE):
                x[pl.ds(r0, cpp), s * KV_WIDTH:(s + 1) * KV_WIDTH] = y[s * cpp:(s + 1) * cpp]
        return carry

    lax.fori_loop(0, PAGES_PER_STEP, page_rows, 0, unroll=4)
    pk_ref[0] = _dot(xk[...], wk_ref[...])
    pv_ref[0] = _dot(xv[...], wv_ref[...])


def _cmp_part_paged(page_table, poolk, poolv, wk, wv):
    nb, n_pages = page_table.shape
    cpp = PAGE_SIZE // CMP_STRIDE
    assert cpp == SUBLANES
    n = wk.shape[1]
    rows = PAGES_PER_STEP * cpp
    slot = jnp.arange(PAGE_SIZE)
    perm = (slot[None, :] == (slot[:, None] % cpp) * CMP_STRIDE + slot[:, None] // cpp).astype(MXU_DT)
    any_spec = pl.BlockSpec(memory_space=pl.ANY)
    full = lambda a: pl.BlockSpec(a.shape, lambda b, c, pt: (0, 0))
    os_ = pl.BlockSpec((1, rows, n), lambda b, c, pt: (b, c, 0))
    page_buf = pltpu.VMEM((2, PAGES_PER_STEP, KV_WIDTH, PAGE_SIZE), F32)
    chunk_rows = pltpu.VMEM((rows, CMP_STRIDE * KV_WIDTH), F32)
    grid_spec = pltpu.PrefetchScalarGridSpec(
        num_scalar_prefetch=1, grid=(nb, n_pages // PAGES_PER_STEP),
        in_specs=[any_spec, any_spec, full(perm), full(wk), full(wv)], out_specs=[os_, os_],
        scratch_shapes=[page_buf, page_buf, chunk_rows, chunk_rows, pltpu.SemaphoreType.DMA((2, 2))])
    return pl.pallas_call(
        _cmp_part_paged_kernel, grid_spec=grid_spec,
        out_shape=[jax.ShapeDtypeStruct((nb, n_pages * cpp, n), F32)] * 2,
        compiler_params=_cparams(("arbitrary", "arbitrary")), name="cmp_part_paged",
    )(page_table, poolk, poolv, perm, wk, wv)


def _cmp_finish_kernel(pk_ref, pv_ref, pe_ref, w1k_ref, w1v_ref, w2k_ref, w2v_ref, kc_ref, vc_ref):
    pe = pe_ref[...]
    for p_ref, w1_ref, w2_ref, o_ref in ((pk_ref, w1k_ref, w2k_ref, kc_ref),
                                         (pv_ref, w1v_ref, w2v_ref, vc_ref)):
        p = p_ref[0]
        n_ch = p.shape[0]
        p0, p1 = p[:, :KV_WIDTH], p[:, KV_WIDTH:]
        row = lax.broadcasted_iota(jnp.int32, p1.shape, 0)
        p1_next = jnp.where(row == n_ch - 1, 0.0, pltpu.roll(p1, n_ch - 1, 0))
        bias = jnp.sum(pe * w1_ref[...], axis=0, keepdims=True)
        o_ref[0] = _dot(_silu(p0 + p1_next + bias), w2_ref[...]).astype(o_ref.dtype)


def _cmp_finish(pk, pv, pe_col, w1k_t, w1v_t, w2k_bd, w2v_bd):
    nb, n_ch, n = pk.shape
    ps = pl.BlockSpec((1, n_ch, n), lambda b: (b, 0, 0))
    full = lambda a: pl.BlockSpec(a.shape, lambda b: (0,) * a.ndim)
    os_ = pl.BlockSpec((1, n_ch, KV_WIDTH), lambda b: (b, 0, 0))
    return pl.pallas_call(
        _cmp_finish_kernel, grid=(nb,),
        in_specs=[ps, ps, full(pe_col), full(w1k_t), full(w1v_t), full(w2k_bd), full(w2v_bd)],
        out_specs=[os_, os_],
        out_shape=[jax.ShapeDtypeStruct((nb, n_ch, KV_WIDTH), MXU_DT)] * 2,
        compiler_params=_cparams(("arbitrary",)), name="cmp_finish",
    )(pk, pv, pe_col, w1k_t, w1v_t, w2k_bd, w2v_bd)


def _cmp_weights(w1, w2):
    eye = jnp.eye(N_KV, dtype=F32)
    w1r = w1.reshape(CMP_RATIO, CMP_STRIDE, HEAD_DIM, CMP_HIDDEN)
    big = jnp.einsum('rsdh,gk->sgdrkh', w1r, eye).reshape(CMP_STRIDE * KV_WIDTH, CMP_RATIO * KV_WIDTH)
    w1_t = jnp.tile(w1, (1, N_KV))
    w2_bd = jnp.einsum('hd,gk->ghkd', w2, eye).reshape(N_KV * CMP_HIDDEN, KV_WIDTH)
    return big.astype(MXU_DT), w1_t, w2_bd.astype(MXU_DT)


def _lane_half_mask(shape, half):
    lane = lax.broadcasted_iota(jnp.int32, shape, len(shape) - 1)
    return (lane >= HEAD_DIM) if half else (lane < HEAD_DIM)


def _build_q4(q_ref, q4_ref, tq):
    del tq
    for g in range(N_KV):
        hg = g % 2
        parts = []
        for r in range(GROUP_R):
            h = g * GROUP_R + r
            slab = q_ref[0, :, (h // 2) * LANES:(h // 2 + 1) * LANES].astype(F32)
            if h % 2 != hg:
                slab = pltpu.roll(slab, HEAD_DIM, 1)
            parts.append(jnp.where(_lane_half_mask(slab.shape, hg), slab, 0.0))
        q4_ref[g] = jnp.concatenate(parts, axis=0).astype(q4_ref.dtype)


def _row_qpos(rows, cols, tq, q0):
    t = lax.broadcasted_iota(jnp.int32, (rows, cols), 0) & (tq - 1)
    return q0 + t


def _add_head_bias(s, bias, tq):
    rows, width = s.shape
    return (s.reshape(GROUP_R, tq, width) + bias[None]).reshape(rows, width)


def _softmax_attend(q4, k, v, bias, tq):
    t = _add_head_bias(_dot_nt(q4, k), bias, tq)
    m = jnp.max(t, axis=1, keepdims=True)
    e = jnp.exp2(t - m)
    l = jnp.sum(e, axis=1, keepdims=True)
    inv = jnp.where(m > 0.5 * NEG, 1.0 / l, 0.0)
    return _dot(e, v) * inv, e, inv


def _top_blocks(imp, qpos, n_blocks, axis):
    j = lax.broadcasted_iota(jnp.int32, imp.shape, axis)
    cur = qpos >> SEL_SHIFT
    forced = (j == 0) | (j == cur) | (j == cur - 1)
    valid = j * SEL_LEN <= qpos
    v = jnp.where(valid, imp + jnp.where(forced, SEL_FORCE, 0.0), -SEL_FORCE)
    v = jnp.where(j < n_blocks, v, -jnp.inf)
    sel = jnp.zeros(imp.shape, F32)
    jf = j.astype(F32)
    for _ in range(N_SEL):
        m = jnp.max(v, axis=axis, keepdims=True)
        first = jnp.min(jnp.where(v == m, jf, float(imp.shape[axis])), axis=axis, keepdims=True)
        hit = jf == first
        sel = jnp.where(hit, 1.0, sel)
        v = jnp.where(hit, -jnp.inf, v)
    return sel


def _compressed_and_select(q4_ref, kc_ref, vc_ref, ov_ref, oc_ref, selneg_ref, tq, q0, n_blocks,
                           blocks_on_rows, ncp=None, block_rows=None):
    ncp = kc_ref.shape[1] if ncp is None else ncp
    n = lax.broadcasted_iota(jnp.int32, (tq, ncp), 1)
    bias_c = jnp.where(n * CMP_STRIDE + (CMP_LEN - 1) <= _row_qpos(tq, ncp, tq, q0), 0.0, NEG)
    psums = []
    for g in range(N_KV):
        sl = slice((g // 2) * LANES, (g // 2 + 1) * LANES)
        o, e, inv = _softmax_attend(q4_ref[g], kc_ref[0, :ncp, sl], vc_ref[0, :ncp, sl], bias_c, tq)
        oc_ref[g] = o
        p = e * inv
        ps = p[0:tq]
        for r in range(1, GROUP_R):
            ps = ps + p[r * tq:(r + 1) * tq]
        psums.append(ps)
    psum = jnp.concatenate(psums, axis=0)
    if blocks_on_rows:
        nb = ov_ref.shape[0]
        block_rows = nb if block_rows is None else block_rows
        ov_t = ov_ref[:block_rows, :ncp]
        imp_t = sum(_dot_nt(ov_t, piece) for piece in _split3(psum))
        qpos_t = q0 + (lax.broadcasted_iota(jnp.int32, imp_t.shape, 1) & (tq - 1))
        sel_t = _top_blocks(imp_t, qpos_t, min(n_blocks, block_rows), 0)
        if block_rows < nb:
            sel_t = jnp.concatenate([sel_t, jnp.zeros((nb - block_rows, sel_t.shape[1]), F32)], axis=0)
        sel = sel_t.T
    else:
        imp = _dot_exact_rhs(psum, ov_ref[...])
        sel = _top_blocks(imp, _row_qpos(N_KV * tq, imp.shape[1], tq, q0), n_blocks, 1)
    selneg_ref[...] = jnp.where(sel > 0.5, 0.0, NEG).astype(selneg_ref.dtype)


def _build_q4_masked(q4_ref, selneg_ref, q4w_ref, tq):
    n_windows = q4w_ref.shape[0]
    for g in range(N_KV):
        hg = g % 2
        q4 = q4_ref[g].astype(F32)
        mask_rows = selneg_ref[g * tq:(g + 1) * tq, :].astype(F32)
        for w in range(n_windows):
            slab = mask_rows[:, (w // 2) * LANES:(w // 2 + 1) * LANES]
            if w % 2 == hg:
                slab = pltpu.roll(slab, HEAD_DIM, 1)
            slab = jnp.where(_lane_half_mask(slab.shape, 1 - hg), slab, 0.0)
            q4w_ref[w, g] = (q4 + jnp.concatenate([slab] * GROUP_R, axis=0)).astype(q4w_ref.dtype)


def _selected_tile_update(q4w_ref, ks_ref, vs_ref, key0, first_block, tq, q0, m_ref, acc_ref):
    window = first_block >> SEL_SHIFT
    shape = (KEY_TILE, LANES)
    key_row = lax.broadcasted_iota(jnp.int32, shape, 0)
    lane = lax.broadcasted_iota(jnp.int32, shape, 1)
    block_in_window = first_block - window * HEAD_DIM + (key_row >> SEL_SHIFT)
    block_one_hot = jnp.where((lane & (HEAD_DIM - 1)) == block_in_window, 1.0, 0.0).astype(MXU_DT)
    kpos = key0 + lax.broadcasted_iota(jnp.int32, (tq, KEY_TILE), 1)
    causal = jnp.where(kpos <= _row_qpos(tq, KEY_TILE, tq, q0), 0.0, NEG)
    for g in range(N_KV):
        sl = slice((g // 2) * LANES, (g // 2 + 1) * LANES)
        own = _lane_half_mask(shape, g % 2)
        k1 = jnp.where(own, ks_ref[0, pl.ds(key0, KEY_TILE), sl], block_one_hot)
        v1 = _ones_in_other_half(vs_ref[0, pl.ds(key0, KEY_TILE), sl], g % 2, 1)
        t = _add_head_bias(_dot_nt(q4w_ref[window, g], k1), causal, tq)
        m_old = m_ref[g]
        m_new = jnp.maximum(m_old, jnp.max(t, axis=1, keepdims=True))
        alpha = jnp.exp2(m_old - m_new)
        e = jnp.concatenate([jnp.exp2(t[:, c * LANES:(c + 1) * LANES] - m_new)
                             for c in range(KEY_TILE // LANES)], axis=1).astype(MXU_DT)
        acc_ref[g] = alpha * acc_ref[g] + jnp.dot(e, v1, preferred_element_type=F32)
        m_ref[g] = m_new


def _init_softmax_state(m_ref, acc_ref):
    m_ref[...] = jnp.full(m_ref.shape, NEG, F32)
    acc_ref[...] = jnp.zeros(acc_ref.shape, F32)


def _gate_select_matrix():
    c = jnp.arange(LANES)[None, :, None]
    g = jnp.arange(N_KV)[:, None, None]
    blk = jnp.arange(GROUP_R * 3 * LANES)[None, None, :] // LANES
    sel = c == 3 * (GROUP_R * g + blk // 3) + blk % 3
    return jnp.concatenate([sel, sel], axis=1).astype(MXU_DT)


def _combine_and_store(gl_ref, gsel_ref, oc_ref, ow_ref, selected_of, o_ref, tq):
    gates = _sigmoid(gl_ref[0])
    g_hi = gates.astype(MXU_DT)
    g_lo = (gates - g_hi.astype(F32)).astype(MXU_DT)
    gate_pieces = jnp.concatenate([g_hi, g_lo], axis=1)
    for g in range(N_KV):
        hg = g % 2
        o_s = selected_of(g)
        o_c, o_w = oc_ref[g], ow_ref[g]
        lane_gates = jnp.dot(gate_pieces, gsel_ref[g], preferred_element_type=F32)
        heads = []
        for r in range(GROUP_R):
            h = g * GROUP_R + r
            rs = slice(r * tq, (r + 1) * tq)
            gc = [lane_gates[:, (3 * r + j) * LANES:(3 * r + j + 1) * LANES] for j in range(3)]
            x = gc[0] * o_c[rs] + gc[1] * o_s[rs] + gc[2] * o_w[rs]
            if h % 2 != hg:
                x = pltpu.roll(x, HEAD_DIM, 1)
            heads.append(x)
        for i in range(GROUP_R // 2):
            slab = jnp.where(_lane_half_mask(heads[0].shape, 0), heads[2 * i], heads[2 * i + 1])
            lo = (g * GROUP_R // 2 + i) * LANES
            o_ref[0, :, lo:lo + LANES] = slab


def _window_attend(q4_ref, kw, vw, bias, ow_ref, tq):
    for g in range(N_KV):
        sl = slice((g // 2) * LANES, (g // 2 + 1) * LANES)
        ow_ref[g] = _softmax_attend(q4_ref[g], kw[:, sl], vw[:, sl], bias, tq)[0]


def _attn_scratch(rows, tq, nbp):
    per_group = lambda dt: pltpu.VMEM((N_KV, rows, LANES), dt)
    return [per_group(MXU_DT), per_group(F32), per_group(F32), pltpu.VMEM((N_KV * tq, nbp), MXU_DT),
            per_group(F32), per_group(F32)]


def _ones_in_other_half(v, hg, axis):
    idx = lax.broadcasted_iota(jnp.int32, v.shape, axis)
    own = (idx >= HEAD_DIM) if hg else (idx < HEAD_DIM)
    return jnp.where(own, v, jnp.ones_like(v))


def _attn_prompt_kernel(*refs, tq, n_win):
    (q_ref, gl_ref, gsel_ref, kc_ref, vc_ref, ov_ref, ks_ref, vs_ref) = refs[:8]
    kw_refs = refs[8:8 + n_win]
    vw_refs = refs[8 + n_win:8 + 2 * n_win]
    o_ref = refs[8 + 2 * n_win]
    q4_ref, oc_ref, ow_ref, selneg_ref, m_ref, acc_ref, q4w_ref = refs[9 + 2 * n_win:]
    i = pl.program_id(1)
    q0 = i * tq
    n_blocks = ks_ref.shape[1] // SEL_LEN

    _build_q4(q_ref, q4_ref, tq)
    ncp = kc_ref.shape[1]
    n_variants = max(1, ncp // LANES)
    variant = jnp.minimum((q0 + tq - CMP_STRIDE) // (CMP_STRIDE * LANES), n_variants - 1)
    for v in range(n_variants):
        width = ncp if n_variants == 1 else (v + 1) * LANES
        block_rows = None if n_variants == 1 else width * CMP_STRIDE // SEL_LEN

        @pl.when(variant == v)
        def _():
            _compressed_and_select(q4_ref, kc_ref, vc_ref, ov_ref, oc_ref, selneg_ref, tq, q0, n_blocks, True,
                                   ncp=width, block_rows=block_rows)

    kw = jnp.concatenate([r[0] for r in kw_refs], axis=0)
    vw = jnp.concatenate([r[0] for r in vw_refs], axis=0)
    lw = kw.shape[0]
    kpos = (i - (n_win - 1)) * tq + lax.broadcasted_iota(jnp.int32, (tq, lw), 1)
    qpos = _row_qpos(tq, lw, tq, q0)
    in_window = (kpos <= qpos) & (kpos >= qpos - WINDOW) & (kpos >= 0)
    _window_attend(q4_ref, kw, vw, jnp.where(in_window, 0.0, NEG), ow_ref, tq)

    _init_softmax_state(m_ref, acc_ref)
    _build_q4_masked(q4_ref, selneg_ref, q4w_ref, tq)

    def body(kt, carry):
        k0 = pl.multiple_of(kt * KEY_TILE, KEY_TILE)
        _selected_tile_update(q4w_ref, ks_ref, vs_ref, k0, kt * BLOCKS_PER_TILE, tq, q0, m_ref, acc_ref)
        return carry

    lax.fori_loop(0, (q0 + tq + KEY_TILE - 1) // KEY_TILE, body, 0)

    def selected_of(g):
        acc = acc_ref[g]
        return acc / pltpu.roll(acc, HEAD_DIM, 1)

    _combine_and_store(gl_ref, gsel_ref, oc_ref, ow_ref, selected_of, o_ref, tq)


def _attn_prompt(q, glog, gsel, kc, vc, ov, ks, vs, kw, vw):
    b, t, d = q.shape
    tq = PROMPT_Q_TILE
    n_win = WINDOW // tq + 1
    nbp = ov.shape[0]
    blk = lambda w: pl.BlockSpec((1, tq, w), lambda bi, i: (bi, i, 0))
    per_b = lambda a: pl.BlockSpec((1,) + a.shape[1:], lambda bi, i: (bi, 0, 0))
    resident = lambda a: pl.BlockSpec((1,) + a.shape[1:], lambda bi, i: (bi, 0, 0),
                                      pipeline_mode=pl.Buffered(1))
    win = [pl.BlockSpec((1, tq, KV_WIDTH),
                        functools.partial(lambda bi, i, k: (bi, jnp.maximum(i - (n_win - 1) + k, 0), 0), k=k))
           for k in range(n_win)]
    rows = GROUP_R * tq
    return pl.pallas_call(
        functools.partial(_attn_prompt_kernel, tq=tq, n_win=n_win),
        grid=(b, t // tq),
        in_specs=[blk(d), blk(LANES), pl.BlockSpec(gsel.shape, lambda bi, i: (0, 0, 0)),
                  per_b(kc), per_b(vc), pl.BlockSpec(ov.shape, lambda bi, i: (0, 0)),
                  resident(ks), resident(vs)] + win + win,
        out_specs=blk(d),
        out_shape=jax.ShapeDtypeStruct((b, t, d), F32),
        scratch_shapes=_attn_scratch(rows, tq, nbp)
                       + [pltpu.VMEM((nbp // HEAD_DIM, N_KV, rows, LANES), MXU_DT)],
        compiler_params=_cparams(("arbitrary", "arbitrary")), name="attn_prompt",
    )(q, glog, gsel, kc, vc, ov, ks, vs, *([kw] * n_win), *([vw] * n_win))


def _attn_sample_kernel(pt_ref, q_ref, gl_ref, gsel_ref, kc_ref, vc_ref, ov_ref, kw_ref, vw_ref, kn_ref, vn_ref,
                        poolk_ref, poolv_ref, o_ref,
                        bufk, bufv, sems, q4_ref, oc_ref, ow_ref, selneg_ref, qbd_ref, m_ref, l_ref, acc_ref,
                        *, tq, past, n_chunks):
    b, c = pl.program_id(0), pl.program_id(1)
    q0 = past
    n_blocks = past // SEL_LEN + 1
    rows = GROUP_R * tq

    @pl.when(c < n_chunks)
    def _():
        _gather_pages_pipelined(pt_ref, b, c, n_chunks, (poolk_ref, poolv_ref), (bufk, bufv), sems)

    @pl.when(c == 0)
    def _():
        _build_q4(q_ref, q4_ref, tq)
        for g in range(N_KV):
            own, zero = q4_ref[g], jnp.zeros((rows, LANES), q4_ref.dtype)
            qbd_ref[g * rows:(g + 1) * rows, :] = jnp.concatenate([zero, own] if g // 2 else [own, zero], axis=1)
        _compressed_and_select(q4_ref, kc_ref, vc_ref, ov_ref, oc_ref, selneg_ref, tq, q0, n_blocks, False)
        lw = kw_ref.shape[1]
        kpos = past - WINDOW + lax.broadcasted_iota(jnp.int32, (tq, lw), 1)
        qpos = _row_qpos(tq, lw, tq, q0)
        in_window = (kpos <= qpos) & (kpos >= qpos - WINDOW)
        _window_attend(q4_ref, kw_ref[0], vw_ref[0], jnp.where(in_window, 0.0, NEG), ow_ref, tq)
        m_ref[...] = jnp.full(m_ref.shape, NEG, F32)
        l_ref[...] = jnp.zeros(l_ref.shape, F32)
        acc_ref[...] = jnp.zeros(acc_ref.shape, F32)

    def tile_update(k, v, feature_major, first_block, key0):
        kt = SAMPLE_KEY_TILE
        selneg = selneg_ref[...]
        nb = selneg.shape[1]
        jj = lax.broadcasted_iota(jnp.int32, (nb, kt), 0)
        kk = lax.broadcasted_iota(jnp.int32, (nb, kt), 1)
        expand = jnp.where(jj == first_block + (kk >> SEL_SHIFT), 1.0, 0.0).astype(MXU_DT)
        bias_sel = jnp.dot(selneg, expand, preferred_element_type=F32)
        kpos = key0 + lax.broadcasted_iota(jnp.int32, (tq, kt), 1)
        causal = jnp.where(kpos <= _row_qpos(tq, kt, tq, q0), 0.0, NEG)
        bias = bias_sel.reshape(N_KV, 1, tq, kt) + causal[None, None]
        qbd = qbd_ref[...]
        s = jnp.dot(qbd, k, preferred_element_type=F32) if feature_major else _dot_nt(qbd, k)
        t = (s.reshape(N_KV, GROUP_R, tq, kt) + bias).reshape(N_KV * rows, kt)
        m_old = m_ref[...]
        m_new = jnp.maximum(m_old, jnp.max(t, axis=1, keepdims=True))
        alpha = jnp.exp2(m_old - m_new)
        e = jnp.concatenate([jnp.exp2(t[:, i * LANES:(i + 1) * LANES] - m_new) for i in range(kt // LANES)],
                            axis=1)
        l_ref[...] = alpha * l_ref[...] + jnp.sum(e, axis=1, keepdims=True)
        e = e.astype(MXU_DT)
        pv = _dot_nt(e, v) if feature_major else jnp.dot(e, v, preferred_element_type=F32)
        acc_ref[...] = jnp.concatenate([alpha] * (KV_WIDTH // LANES), axis=1) * acc_ref[...] + pv
        m_ref[...] = m_new

    @pl.when(c < n_chunks)
    def _():
        pages_per_tile = SAMPLE_KEY_TILE // PAGE_SIZE
        slot = (b * n_chunks + c) % 2

        def body(kt, carry):
            p0 = kt * pages_per_tile
            k_t = jnp.concatenate([bufk[slot, p0 + j] for j in range(pages_per_tile)], axis=1).astype(MXU_DT)
            v_t = jnp.concatenate([bufv[slot, p0 + j] for j in range(pages_per_tile)], axis=1).astype(MXU_DT)
            page0 = c * PAGES_PER_STEP + p0
            tile_update(k_t, v_t, True, page0 * (PAGE_SIZE // SEL_LEN), page0 * PAGE_SIZE)
            return carry

        lax.fori_loop(0, PAGES_PER_STEP // pages_per_tile, body, 0)

    @pl.when(c == n_chunks)
    def _():
        tile_update(kn_ref[0], vn_ref[0], False, past // SEL_LEN, past)
        inv = 1.0 / l_ref[...]

        def selected_of(g):
            sl = slice((g // 2) * LANES, (g // 2 + 1) * LANES)
            return acc_ref[g * rows:(g + 1) * rows, sl] * inv[g * rows:(g + 1) * rows]

        _combine_and_store(gl_ref, gsel_ref, oc_ref, ow_ref, selected_of, o_ref, tq)


def _attn_sample(page_table, q, glog, gsel, kc, vc, ov, kw_full, vw_full, k_new, v_new, poolk, poolv):
    b, tq, d = q.shape
    n_pages = page_table.shape[1]
    past = n_pages * PAGE_SIZE
    n_chunks = n_pages // PAGES_PER_STEP
    rows = GROUP_R * tq
    nbp = ov.shape[1]
    per_b = lambda a: pl.BlockSpec((1,) + a.shape[1:], lambda bi, c, pt: (bi, 0, 0))
    any_spec = pl.BlockSpec(memory_space=pl.ANY)
    grid_spec = pltpu.PrefetchScalarGridSpec(
        num_scalar_prefetch=1, grid=(b, n_chunks + 1),
        in_specs=[per_b(q), per_b(glog), pl.BlockSpec(gsel.shape, lambda bi, c, pt: (0, 0, 0)),
                  per_b(kc), per_b(vc), pl.BlockSpec(ov.shape, lambda bi, c, pt: (0, 0)),
                  per_b(kw_full), per_b(vw_full), per_b(k_new), per_b(v_new), any_spec, any_spec],
        out_specs=per_b(q),
        scratch_shapes=[pltpu.VMEM((2, PAGES_PER_STEP, KV_WIDTH, PAGE_SIZE), F32),
                        pltpu.VMEM((2, PAGES_PER_STEP, KV_WIDTH, PAGE_SIZE), F32),
                        pltpu.SemaphoreType.DMA((2, 2))] + _attn_scratch(rows, tq, nbp)[:4]
                       + [pltpu.VMEM((N_KV * rows, KV_WIDTH), MXU_DT), pltpu.VMEM((N_KV * rows, LANES), F32),
                          pltpu.VMEM((N_KV * rows, LANES), F32), pltpu.VMEM((N_KV * rows, KV_WIDTH), F32)])
    return pl.pallas_call(
        functools.partial(_attn_sample_kernel, tq=tq, past=past, n_chunks=n_chunks),
        grid_spec=grid_spec,
        out_shape=jax.ShapeDtypeStruct((b, tq, d), F32),
        compiler_params=_cparams(("arbitrary", "arbitrary")), name="attn_sample",
    )(page_table, q, glog, gsel, kc, vc, ov, kw_full, vw_full, k_new, v_new, poolk, poolv)


def _win_update_kernel(ck_ref, cv_ref, nk_ref, nv_ref, ok_ref, ov_ref, fk_ref, fv_ref):
    t = nk_ref.shape[1]
    wb = ck_ref.shape[1]
    for c_ref, n_ref, o_ref, f_ref in ((ck_ref, nk_ref, ok_ref, fk_ref), (cv_ref, nv_ref, ov_ref, fv_ref)):
        o_ref[0, 0:wb - t, :] = c_ref[0, t:wb, :]
        o_ref[0, wb - t:wb, :] = n_ref[0]
        pad = jnp.zeros((f_ref.shape[1] - wb - t, f_ref.shape[2]), F32)
        f_ref[0] = jnp.concatenate([c_ref[0], n_ref[0], pad], axis=0).astype(f_ref.dtype)


def _win_update(cache_k, cache_v, new_k, new_v):
    b, wb, w = cache_k.shape
    t = new_k.shape[1]
    full_len = wb + LANES
    cs = pl.BlockSpec((1, wb, w), lambda i: (i, 0, 0))
    ns = pl.BlockSpec((1, t, w), lambda i: (i, 0, 0))
    fs = pl.BlockSpec((1, full_len, w), lambda i: (i, 0, 0))
    return pl.pallas_call(
        _win_update_kernel, grid=(b,), in_specs=[cs, cs, ns, ns], out_specs=[cs, cs, fs, fs],
        out_shape=[jax.ShapeDtypeStruct((b, wb, w), F32)] * 2
                  + [jax.ShapeDtypeStruct((b, full_len, w), MXU_DT)] * 2,
        compiler_params=_cparams(("arbitrary",)), name="win_update",
    )(cache_k, cache_v, new_k, new_v)


def _overlap_matrix(n_cmp_pad, n_cmp, n_blocks_pad, n_blocks):
    n = jnp.arange(n_cmp_pad)[:, None]
    j = jnp.arange(n_blocks_pad)[None, :]
    c_start, s_start = n * CMP_STRIDE, j * SEL_LEN
    ov = (c_start < s_start + SEL_LEN) & (c_start + CMP_LEN > s_start) & (n < n_cmp) & (j < n_blocks)
    return ov.astype(MXU_DT)


def _finish_rows(a, w_ref, gpost_ref, x_ref, gate_ref, y_ref):
    o = jnp.dot(a.astype(MXU_DT), w_ref[...], preferred_element_type=F32)
    n = o * lax.rsqrt(jnp.mean(o * o, axis=-1, keepdims=True) + RMS_EPS) * gpost_ref[...]
    y_ref[0] = x_ref[0] + gate_ref[0] * n


def _attn_out_kernel(o_ref, z_ref, w_ref, gpost_ref, x_ref, gate_ref, y_ref):
    _finish_rows(o_ref[0] * _silu(z_ref[0]), w_ref, gpost_ref, x_ref, gate_ref, y_ref)


def _ssm_out_kernel(y_ref, z_ref, wglu_ref, w_ref, gpost_ref, x_ref, gate_ref, o_ref):
    v = jnp.dot(y_ref[0].astype(MXU_DT), wglu_ref[...], preferred_element_type=F32)
    half = v.shape[1] // 2
    glu = v[:, :half] * _sigmoid(v[:, half:])
    _finish_rows(glu * _silu(z_ref[0]), w_ref, gpost_ref, x_ref, gate_ref, o_ref)


def _mixer_out(body, name, acts, weights, g_post, x, gate):
    b, t, d = x.shape
    tm = min(t, 256)
    row = pl.BlockSpec((1, tm, d), lambda i, j: (i, j, 0))
    full = lambda a: pl.BlockSpec(a.shape, lambda i, j: (0,) * a.ndim)
    per_b = pl.BlockSpec((1, 1, d), lambda i, j: (i, 0, 0))
    return pl.pallas_call(
        body, grid=(b, t // tm),
        in_specs=[row] * len(acts) + [full(w) for w in weights] + [full(g_post), row, per_b],
        out_specs=row, out_shape=jax.ShapeDtypeStruct((b, t, d), F32),
        compiler_params=_cparams(("arbitrary", "arbitrary")), name=name,
    )(*acts, *weights, g_post, x, gate)


def _ssm_proj_kernel(x_ref, shift_ref, scale_ref, g_ref, w_ref, u_ref, z_ref):
    h = _rms_mod(x_ref[0], g_ref[...], scale_ref[0], shift_ref[0]).astype(MXU_DT)
    p = jnp.dot(h, w_ref[...], preferred_element_type=F32)
    half = p.shape[1] // 2
    u_ref[0] = p[:, :half]
    z_ref[0] = p[:, half:]


def _ssm_proj(x, shift, scale, g_pre, w):
    b, t, d = x.shape
    tm = min(t, 256)
    row = pl.BlockSpec((1, tm, d), lambda i, j: (i, j, 0))
    per_b = pl.BlockSpec((1, 1, d), lambda i, j: (i, 0, 0))
    full = lambda a: pl.BlockSpec(a.shape, lambda i, j: (0,) * a.ndim)
    out = jax.ShapeDtypeStruct((b, t, d), F32)
    return pl.pallas_call(
        _ssm_proj_kernel, grid=(b, t // tm),
        in_specs=[row, per_b, per_b, full(g_pre), full(w)], out_specs=[row, row], out_shape=[out, out],
        compiler_params=_cparams(("arbitrary", "arbitrary")), name="ssm_proj",
    )(x, shift, scale, g_pre, w)


def _s5_prep_kernel(ar_ref, ai_ref, ldt_ref, bre_ref, bim_ref, lre_ref, lim_ref, cbre_ref, cbim_ref):
    ar, ai = ar_ref[...], ai_ref[...]
    dt = jnp.exp(ldt_ref[...])
    mag = jnp.exp(dt * ar)
    lam_re, lam_im = mag * jnp.cos(dt * ai), mag * jnp.sin(dt * ai)
    den = ar * ar + ai * ai
    nr, ni = lam_re - 1.0, lam_im
    coef_re, coef_im = (nr * ar + ni * ai) / den, (ni * ar - nr * ai) / den
    lre_ref[...] = lam_re
    lim_ref[...] = lam_im
    cr, ci = coef_re[:, None, :], coef_im[:, None, :]
    b_re, b_im = bre_ref[...], bim_ref[...]
    cbre_ref[...] = cr * b_re - ci * b_im
    cbim_ref[...] = cr * b_im + ci * b_re


def _s5_prep(a_re, a_im, log_dt, b_re_t, b_im_t):
    ng, p = a_re.shape
    full = lambda a: pl.BlockSpec(a.shape, lambda: (0,) * a.ndim)
    args = (a_re, a_im, log_dt.reshape(ng, 1), b_re_t, b_im_t)
    lam = jax.ShapeDtypeStruct((ng, p), F32)
    cb = jax.ShapeDtypeStruct(b_re_t.shape, F32)
    return pl.pallas_call(
        _s5_prep_kernel, in_specs=[full(a) for a in args],
        out_specs=[full(lam), full(lam), full(cb), full(cb)], out_shape=[lam, lam, cb, cb],
        name="s5_prep",
    )(*args)


SCAN_LANES = 512


def _s5_scan_kernel(u_ref, d_ref, bre_ref, bim_ref, cre_ref, cim_ref, lre_ref, lim_ref, h0re_ref, h0im_ref,
                    y_ref, hre_ref, him_ref, sre, sim, *, n_batch, nbp, tt):
    ti = pl.program_id(1)
    n_lc = sre.shape[0]

    @pl.when(ti == 0)
    def _():
        hre_ref[...] = h0re_ref[...]
        him_ref[...] = h0im_ref[...]
        if nbp != n_batch:
            sre[...] = jnp.zeros(sre.shape, F32)
            sim[...] = jnp.zeros(sim.shape, F32)

    def drive(b, carry):
        ub = u_ref[b].astype(MXU_DT)
        for s_ref, w_ref in ((sre, bre_ref), (sim, bim_ref)):
            bb = jnp.dot(ub, w_ref[0], preferred_element_type=F32)
            for c in range(n_lc):
                s_ref[c, pl.ds(b, tt, stride=nbp), :] = bb[:, c * LANES:(c + 1) * LANES]
        return carry

    lax.fori_loop(0, n_batch, drive, 0)

    per_scan = SCAN_LANES // LANES
    for sb in range(nbp // SUBLANES):
        for lc in range(n_lc // per_scan):
            chunks = range(lc * per_scan, (lc + 1) * per_scan)
            srow = slice(sb * SUBLANES, (sb + 1) * SUBLANES)
            lane = lambda c: slice(c * LANES, (c + 1) * LANES)
            lam_re = [jnp.broadcast_to(lre_ref[:, lane(c)], (SUBLANES, LANES)) for c in chunks]
            lam_im = [jnp.broadcast_to(lim_ref[:, lane(c)], (SUBLANES, LANES)) for c in chunks]

            def step(t, carry):
                r0 = pl.multiple_of(t * nbp + sb * SUBLANES, SUBLANES)
                new = []
                for k, c in enumerate(chunks):
                    h_re, h_im = carry[k]
                    n_re = lam_re[k] * h_re - lam_im[k] * h_im + sre[c, pl.ds(r0, SUBLANES), :]
                    n_im = lam_re[k] * h_im + lam_im[k] * h_re + sim[c, pl.ds(r0, SUBLANES), :]
                    sre[c, pl.ds(r0, SUBLANES), :] = n_re
                    sim[c, pl.ds(r0, SUBLANES), :] = n_im
                    new.append((n_re, n_im))
                return tuple(new)

            init = tuple((hre_ref[srow, lane(c)], him_ref[srow, lane(c)]) for c in chunks)
            final = lax.fori_loop(0, tt, step, init, unroll=min(tt, 8))
            for k, c in enumerate(chunks):
                hre_ref[srow, lane(c)] = final[k][0]
                him_ref[srow, lane(c)] = final[k][1]

    def readout(b, carry):
        rows = lambda s_ref: jnp.concatenate(
            [s_ref[c, pl.ds(b, tt, stride=nbp), :] for c in range(n_lc)], axis=1).astype(MXU_DT)
        y = (jnp.dot(rows(sre), cre_ref[0], preferred_element_type=F32)
             - jnp.dot(rows(sim), cim_ref[0], preferred_element_type=F32))
        y_ref[b] = y + d_ref[...] * u_ref[b]
        return carry

    lax.fori_loop(0, n_batch, readout, 0)


def _s5_scan(u, d_skip, b_re_bd, b_im_bd, c_re_bd, c_im_bd, lam_re, lam_im, h0_re, h0_im, nbp):
    b, t, w = u.shape
    q, wq, sq = b_re_bd.shape
    ns = q * sq
    tt = min(t, 256)
    ublk = pl.BlockSpec((b, tt, wq), lambda k, i: (0, i, k))
    dblk = pl.BlockSpec((1, wq), lambda k, i: (0, k))
    bblk = pl.BlockSpec((1, wq, sq), lambda k, i: (k, 0, 0))
    cblk = pl.BlockSpec((1, sq, wq), lambda k, i: (k, 0, 0))
    lblk = pl.BlockSpec((1, sq), lambda k, i: (0, k))
    hblk = pl.BlockSpec((nbp, sq), lambda k, i: (0, k))
    st = jax.ShapeDtypeStruct((nbp, ns), F32)
    return pl.pallas_call(
        functools.partial(_s5_scan_kernel, n_batch=b, nbp=nbp, tt=tt),
        grid=(q, t // tt),
        in_specs=[ublk, dblk, bblk, bblk, cblk, cblk, lblk, lblk, hblk, hblk],
        out_specs=[ublk, hblk, hblk],
        out_shape=[jax.ShapeDtypeStruct((b, t, w), F32), st, st],
        scratch_shapes=[pltpu.VMEM((sq // LANES, tt * nbp, LANES), F32)] * 2,
        compiler_params=_cparams(("arbitrary", "arbitrary")), name="s5_scan",
    )(u, d_skip, b_re_bd, b_im_bd, c_re_bd, c_im_bd, lam_re, lam_im, h0_re, h0_im)


def _s5_block_diag(cb_re, cb_im, c_re, c_im):
    ng, c, p = cb_re.shape
    gq = ng // S5_QUARTERS
    eye = jnp.eye(gq, dtype=F32)

    def drive(w):
        w4 = w.reshape(S5_QUARTERS, gq, c, p)
        return jnp.einsum('qgcp,gk->qgckp', w4, eye).reshape(S5_QUARTERS, gq * c, gq * p).astype(MXU_DT)

    def readout(w):
        w4 = w.reshape(S5_QUARTERS, gq, c, p)
        return jnp.einsum('qgcp,gk->qgpkc', w4, eye).reshape(S5_QUARTERS, gq * p, gq * c).astype(MXU_DT)

    return drive(cb_re), drive(cb_im), readout(c_re), readout(c_im)


def _nsa_layer(xp, xs, modp, mods, caches, page_table, g_pre, g_post, w_in, pe, wk1, wk2, wv1, wv2, w_out):
    cache_kc, cache_vc, cache_ks, cache_vs, cache_kw, cache_vw = caches
    b, t, d = xp.shape
    sb, st, _ = xs.shape
    n_pages = page_table.shape[1]
    past = n_pages * PAGE_SIZE
    assert st < CMP_STRIDE and n_pages % PAGES_PER_STEP == 0 and t % KEY_TILE == 0

    aw = N_HEADS * HEAD_DIM
    cuts = [aw, aw + 6 * KV_WIDTH, aw + 6 * KV_WIDTH + 3 * N_HEADS]
    wq = w_in[:, :cuts[0]].astype(MXU_DT)
    wkv = w_in[:, cuts[0]:cuts[1]].astype(MXU_DT)
    wg = jnp.pad(w_in[:, cuts[1]:cuts[2]], ((0, 0), (0, LANES - 3 * N_HEADS))).astype(MXU_DT)
    wz = w_in[:, cuts[2]:].astype(MXU_DT)
    g_pre2, g_post2 = g_pre.reshape(1, d), g_post.reshape(1, d)

    proj_p = _attn_proj(xp, modp[0], modp[1], g_pre2, _rope_tables(jnp.arange(t)), wq, wkv, wg, wz)
    proj_s = _attn_proj(xs, mods[0], mods[1], g_pre2, _rope_tables(past + jnp.arange(st)), wq, wkv, wg, wz)
    (q_p, kc_p, vc_p, ks_p, vs_p, kw_p, vw_p, ksb_p, vsb_p, kwb_p, vwb_p, gl_p, z_p) = proj_p
    (q_s, kc_s, vc_s, ks_s, vs_s, kw_s, vw_s, ksb_s, vsb_s, _, _, gl_s, z_s) = proj_s

    wk_big, w1k_t, w2k_bd = _cmp_weights(wk1, wk2)
    wv_big, w1v_t, w2v_bd = _cmp_weights(wv1, wv2)
    pe_col = pe.reshape(CMP_LEN * HEAD_DIM, 1)
    chunk_w = CMP_STRIDE * KV_WIDTH

    n_ch = t // CMP_STRIDE
    pk, pv = _cmp_part(kc_p.reshape(b * n_ch, chunk_w), vc_p.reshape(b * n_ch, chunk_w), wk_big, wv_big)
    kcc_p, vcc_p = _cmp_finish(pk.reshape(b, n_ch, -1), pv.reshape(b, n_ch, -1), pe_col,
                               w1k_t, w1v_t, w2k_bd, w2v_bd)
    ns_p = -(-t // SEL_LEN)
    nbp_p = -(-ns_p // LANES) * LANES
    ov_p = _overlap_matrix(n_ch, n_ch - CMP_RATIO + 1, nbp_p, ns_p).T
    gsel = _gate_select_matrix()
    o_p = _attn_prompt(q_p, gl_p, gsel, kcc_p, vcc_p, ov_p, ksb_p, vsb_p, kwb_p, vwb_p)
    y_p = _mixer_out(_attn_out_kernel, "attn_out", (o_p, z_p), (w_out.astype(MXU_DT),), g_post2, xp, modp[2])

    cpp = PAGE_SIZE // CMP_STRIDE
    n_phys = cache_kc.shape[0]
    feature_major = lambda pool: pool.transpose(0, 2, 3, 1).reshape(n_phys, KV_WIDTH, PAGE_SIZE)
    pk, pv = _cmp_part_paged(page_table, feature_major(cache_kc), feature_major(cache_vc), wk_big, wv_big)
    kcc_s, vcc_s = _cmp_finish(pk, pv, pe_col, w1k_t, w1v_t, w2k_bd, w2v_bd)
    n_ch_s = (past + st) // CMP_STRIDE
    ns_s = -(-(past + st) // SEL_LEN)
    nbp_s = -(-ns_s // LANES) * LANES
    ov_s = _overlap_matrix(n_pages * cpp, n_ch_s - CMP_RATIO + 1, nbp_s, ns_s)
    wb = cache_kw.shape[1]
    win_k, win_v, kw_full, vw_full = _win_update(cache_kw.reshape(sb, wb, KV_WIDTH),
                                                 cache_vw.reshape(sb, wb, KV_WIDTH), kw_s, vw_s)
    pad_new = lambda a: jnp.pad(a, ((0, 0), (0, SAMPLE_KEY_TILE - st), (0, 0)))
    o_s = _attn_sample(page_table, q_s, gl_s, gsel, kcc_s, vcc_s, ov_s, kw_full, vw_full,
                       pad_new(ksb_s), pad_new(vsb_s), feature_major(cache_ks), feature_major(cache_vs))
    y_s = _mixer_out(_attn_out_kernel, "attn_out_s", (o_s, z_s), (w_out.astype(MXU_DT),), g_post2, xs, mods[2])

    heads = lambda a: a.reshape(a.shape[0], a.shape[1], N_KV, HEAD_DIM)
    wl = min(WINDOW, t)
    rows_p = (kc_p, vc_p, ks_p, vs_p, kw_p[:, t - wl:], vw_p[:, t - wl:])
    rows_s = (kc_s, vc_s, ks_s, vs_s, win_k, win_v)
    return y_p, y_s, [heads(a) for a in rows_p], [heads(a) for a in rows_s]


def _s5_layer(xp, xs, modp, mods, state_re, state_im, g_pre, g_post, w_in, a_re, a_im, log_dt,
              b_re, b_im, c_re, c_im, d_skip, w_glu, w_out):
    b, t, d = xp.shape
    sb, st, _ = xs.shape
    ng, p, c = b_re.shape
    g_pre2, g_post2 = g_pre.reshape(1, d), g_post.reshape(1, d)
    w_in16, w_glu16, w_out16 = w_in.astype(MXU_DT), w_glu.astype(MXU_DT), w_out.astype(MXU_DT)
    lam_re, lam_im, cb_re, cb_im = _s5_prep(a_re, a_im, log_dt, b_re.transpose(0, 2, 1), b_im.transpose(0, 2, 1))
    bd = _s5_block_diag(cb_re, cb_im, c_re, c_im)
    lam = (lam_re.reshape(1, ng * p), lam_im.reshape(1, ng * p))
    d2 = d_skip.reshape(1, d)

    outs = []
    for x, mod, h0 in ((xp, modp, None), (xs, mods, (state_re, state_im))):
        nb = x.shape[0]
        nbp = -(-nb // SUBLANES) * SUBLANES
        u, z = _ssm_proj(x, mod[0], mod[1], g_pre2, w_in16)
        if h0 is None:
            h0_re = h0_im = jnp.zeros((nbp, ng * p), F32)
        else:
            h0_re, h0_im = (jnp.pad(h.reshape(nb, ng * p), ((0, nbp - nb), (0, 0))) for h in h0)
        y, h_re, h_im = _s5_scan(u, d2, *bd, *lam, h0_re, h0_im, nbp)
        out = _mixer_out(_ssm_out_kernel, "ssm_out", (y, z), (w_glu16, w_out16), g_post2, x, mod[2])
        outs.append((out, h_re[:nb].reshape(nb, ng, p), h_im[:nb].reshape(nb, ng, p)))
    return outs


def kernel(x_prompt, x_sample, cache_k_cmp, cache_v_cmp, cache_k_sel, cache_v_sel, cache_k_win, cache_v_win, state_s5_re, state_s5_im, page_table, c_prompt, c_sample, norm_pre, norm_post, w_ada, b_ada, w_in_attn, pe_cmp, w_cmp_k1, w_cmp_k2, w_cmp_v1, w_cmp_v2, w_out_attn, w_in_ssm, s5_a_re, s5_a_im, s5_log_dt, s5_b_re, s5_b_im, s5_c_re, s5_c_im, s5_d, w_glu, w_out_ssm):
    b, t, d = x_prompt.shape
    sb = x_sample.shape[0]
    depth = w_ada.shape[0]
    c_all = jnp.concatenate([c_prompt, c_sample], axis=0)
    n_c = -(-c_all.shape[0] // SUBLANES) * SUBLANES
    mod = _ada(jnp.pad(c_all, ((0, n_c - c_all.shape[0]), (0, 0))), w_ada, b_ada)

    yp, ys = x_prompt, x_sample
    attn_p, attn_s, ssm_p, ssm_s = [], [], [], []
    for i in range(depth):
        parts = [mod[i, :, k * d:(k + 1) * d] for k in range(3)]
        modp = [a[:b].reshape(b, 1, d) for a in parts]
        mods = [a[b:b + sb].reshape(sb, 1, d) for a in parts]
        li = i // 2
        if i % 2 == 0:
            caches = (cache_k_cmp[li], cache_v_cmp[li], cache_k_sel[li], cache_v_sel[li],
                      cache_k_win[li], cache_v_win[li])
            yp, ys, rows_p, rows_s = _nsa_layer(
                yp, ys, modp, mods, caches, page_table, norm_pre[i], norm_post[i], w_in_attn[li], pe_cmp[li],
                w_cmp_k1[li], w_cmp_k2[li], w_cmp_v1[li], w_cmp_v2[li], w_out_attn[li])
            attn_p.append(rows_p)
            attn_s.append(rows_s)
        else:
            (yp, pr, pi), (ys, sr, si) = _s5_layer(
                yp, ys, modp, mods, state_s5_re[li], state_s5_im[li], norm_pre[i], norm_post[i], w_in_ssm[li],
                s5_a_re[li], s5_a_im[li], s5_log_dt[li], s5_b_re[li], s5_b_im[li], s5_c_re[li], s5_c_im[li],
                s5_d[li], w_glu[li], w_out_ssm[li])
            ssm_p.append((pr, pi))
            ssm_s.append((sr, si))

    outs = [yp, ys]
    for n in range(6):
        outs.append(jnp.stack([rows[n] for rows in attn_p]))
        outs.append(jnp.stack([rows[n] for rows in attn_s]))
    for n in range(2):
        outs.append(jnp.stack([st[n] for st in ssm_p]))
        outs.append(jnp.stack([st[n] for st in ssm_s]))
    return tuple(outs)
```

```python
import functools
import math

import jax
import jax.numpy as jnp
from jax import lax
from jax.experimental import pallas as pl
from jax.experimental.pallas import tpu as pltpu

N_HEADS = 16
HEAD_DIM = 64
N_KV = 4
GROUP_R = N_HEADS // N_KV
ROT_HALF = HEAD_DIM // 8
ROPE_THETA = 500000.0
CMP_LEN = 32
CMP_STRIDE = 16
CMP_RATIO = CMP_LEN // CMP_STRIDE
CMP_HIDDEN = 64
SEL_LEN = 64
SEL_SHIFT = SEL_LEN.bit_length() - 1
N_SEL = 16
WINDOW = 512
PAGE_SIZE = 128
S5_GROUP = 16
S5_STATE = 64
NEG = -1e30
LOG2_E = math.log2(math.e)
SEL_FORCE = 1e4
RMS_EPS = 1e-6
KV_WIDTH = N_KV * HEAD_DIM

LANES = 128
SUBLANES = 8
VMEM_LIMIT_BYTES = 56 * 1024 * 1024

MXU_DT = jnp.bfloat16
F32 = jnp.float32

PROMPT_Q_TILE = 256
KEY_TILE = 512
BLOCKS_PER_TILE = KEY_TILE // SEL_LEN
SAMPLE_KEY_TILE = 1024
PAGES_PER_STEP = 32
S5_QUARTERS = 4


def _cparams(sem):
    return pltpu.CompilerParams(dimension_semantics=sem, vmem_limit_bytes=VMEM_LIMIT_BYTES)


def _dot(a, b):
    return jnp.dot(a.astype(MXU_DT), b.astype(MXU_DT), preferred_element_type=F32)


def _dot_nt(a, b):
    return lax.dot_general(a.astype(MXU_DT), b.astype(MXU_DT), (((1,), (1,)), ((), ())),
                           preferred_element_type=F32)


def _split3(x):
    hi = x.astype(MXU_DT)
    r1 = x - hi.astype(F32)
    mid = r1.astype(MXU_DT)
    lo = (r1 - mid.astype(F32)).astype(MXU_DT)
    return hi, mid, lo


def _dot_exact_rhs(x, m):
    hi, mid, lo = _split3(x)
    return (jnp.dot(hi, m, preferred_element_type=F32) + jnp.dot(mid, m, preferred_element_type=F32)
            + jnp.dot(lo, m, preferred_element_type=F32))


def _dot_f32(a, b):
    ah, am, _ = _split3(a)
    bh, bm, _ = _split3(b)
    d = lambda p, q: jnp.dot(p, q, preferred_element_type=F32)
    return d(ah, bh) + (d(ah, bm) + d(am, bh)) + d(am, bm)


def _sigmoid(x):
    return 1.0 / (1.0 + jnp.exp(-x))


def _silu(x):
    return x * _sigmoid(x)


def _ada_kernel(c_ref, w_ref, b_ref, o_ref):
    o_ref[...] = _dot_f32(c_ref[...], w_ref[...]) + b_ref[...]


def _ada(c_all, w_ada, b_ada):
    depth, d, d3 = w_ada.shape
    n = c_all.shape[0]
    return pl.pallas_call(
        _ada_kernel,
        grid=(depth, d3 // d),
        in_specs=[pl.BlockSpec((n, d), lambda i, j: (0, 0)),
                  pl.BlockSpec((None, d, d), lambda i, j: (i, 0, j)),
                  pl.BlockSpec((None, 1, d), lambda i, j: (i, 0, j))],
        out_specs=pl.BlockSpec((None, n, d), lambda i, j: (i, 0, j)),
        out_shape=jax.ShapeDtypeStruct((depth, n, d3), F32),
        compiler_params=_cparams(("arbitrary", "arbitrary")),
        name="ada_mod",
    )(c_all, w_ada, b_ada.reshape(depth, 1, d3))


def _rms_mod(x, g, scale, shift):
    y = x * lax.rsqrt(jnp.mean(x * x, axis=-1, keepdims=True) + RMS_EPS)
    return (y * g) * (1.0 + scale) + shift


def _rope_slab(x, c, s1, s2):
    return x * c + pltpu.roll(x, ROT_HALF, 1) * s1 + pltpu.roll(x, LANES - ROT_HALF, 1) * s2


def _attn_proj_kernel(x_ref, shift_ref, scale_ref, g_ref, c_ref, s1_ref, s2_ref,
                      wq_ref, wkv_ref, wg_ref, wz_ref,
                      q_ref, kc_ref, vc_ref, ks_ref, vs_ref, kw_ref, vw_ref,
                      ksb_ref, vsb_ref, kwb_ref, vwb_ref, gl_ref, z_ref):
    h = _rms_mod(x_ref[0], g_ref[...], scale_ref[0], shift_ref[0]).astype(MXU_DT)
    c, s1, s2 = c_ref[...], s1_ref[...], s2_ref[...]
    q = jnp.dot(h, wq_ref[...], preferred_element_type=F32)
    qscale = HEAD_DIM ** -0.5 * LOG2_E
    for j in range(q.shape[1] // LANES):
        sl = slice(j * LANES, (j + 1) * LANES)
        q_ref[0, :, sl] = (_rope_slab(q[:, sl], c, s1, s2) * qscale).astype(q_ref.dtype)
    kv = jnp.dot(h, wkv_ref[...], preferred_element_type=F32)
    outs = ((kc_ref, None, False), (vc_ref, None, False), (ks_ref, ksb_ref, True),
            (vs_ref, vsb_ref, False), (kw_ref, kwb_ref, True), (vw_ref, vwb_ref, False))
    for n, (o_ref, ob_ref, roped) in enumerate(outs):
        for j in range(KV_WIDTH // LANES):
            lo = n * KV_WIDTH + j * LANES
            v = kv[:, lo:lo + LANES]
            if roped:
                v = _rope_slab(v, c, s1, s2)
            o_ref[0, :, j * LANES:(j + 1) * LANES] = v
            if ob_ref is not None:
                ob_ref[0, :, j * LANES:(j + 1) * LANES] = v.astype(ob_ref.dtype)
    gl_ref[0] = jnp.dot(h, wg_ref[...], preferred_element_type=F32)
    z_ref[0] = jnp.dot(h, wz_ref[...], preferred_element_type=F32)


def _attn_proj(x, shift, scale, g_pre, tables, wq, wkv, wg, wz):
    b, t, d = x.shape
    tm = min(t, 256)
    cos_t, s1_t, s2_t = tables
    row = lambda i, j: (i, j, 0)
    per_b = pl.BlockSpec((1, 1, d), lambda i, j: (i, 0, 0))
    full = lambda a: pl.BlockSpec(a.shape, lambda i, j: (0,) * a.ndim)
    tab = pl.BlockSpec((tm, LANES), lambda i, j: (j, 0))
    kv32 = jax.ShapeDtypeStruct((b, t, KV_WIDTH), F32)
    kv16 = jax.ShapeDtypeStruct((b, t, KV_WIDTH), MXU_DT)
    kvspec = pl.BlockSpec((1, tm, KV_WIDTH), row)
    return pl.pallas_call(
        _attn_proj_kernel,
        grid=(b, t // tm),
        in_specs=[pl.BlockSpec((1, tm, d), row), per_b, per_b, full(g_pre), tab, tab, tab,
                  full(wq), full(wkv), full(wg), full(wz)],
        out_specs=[pl.BlockSpec((1, tm, d), row)] + [kvspec] * 10
                  + [pl.BlockSpec((1, tm, LANES), row), pl.BlockSpec((1, tm, d), row)],
        out_shape=[jax.ShapeDtypeStruct((b, t, d), MXU_DT)] + [kv32] * 6 + [kv16] * 4
                  + [jax.ShapeDtypeStruct((b, t, LANES), F32), jax.ShapeDtypeStruct((b, t, d), F32)],
        compiler_params=_cparams(("arbitrary", "arbitrary")),
        name="attn_proj",
    )(x, shift, scale, g_pre, cos_t, s1_t, s2_t, wq, wkv, wg, wz)


def _rope_tables(pos):
    inv = ROPE_THETA ** (-jnp.arange(ROT_HALF, dtype=F32) / ROT_HALF)
    ang = pos.astype(F32)[:, None] * inv[None]
    cos, sin = jnp.cos(ang), jnp.sin(ang)
    n = pos.shape[0]
    rest = HEAD_DIM - 2 * ROT_HALF
    zeros = lambda w: jnp.zeros((n, w), F32)
    c64 = jnp.concatenate([cos, cos, jnp.ones((n, rest), F32)], axis=1)
    s1 = jnp.concatenate([zeros(ROT_HALF), sin, zeros(rest)], axis=1)
    s2 = jnp.concatenate([-sin, zeros(ROT_HALF + rest)], axis=1)
    rep = LANES // HEAD_DIM
    return tuple(jnp.tile(a, (1, rep)) for a in (c64, s1, s2))


def _cmp_part_kernel(xk_ref, xv_ref, wk_ref, wv_ref, pk_ref, pv_ref):
    pk_ref[...] = _dot(xk_ref[...], wk_ref[...])
    pv_ref[...] = _dot(xv_ref[...], wv_ref[...])


def _cmp_part(xk, xv, wk, wv):
    rows, width = xk.shape
    tm = min(rows, 256)
    n = wk.shape[1]
    xs = pl.BlockSpec((tm, width), lambda i: (i, 0))
    ws = pl.BlockSpec(wk.shape, lambda i: (0, 0))
    os_ = pl.BlockSpec((tm, n), lambda i: (i, 0))
    return pl.pallas_call(
        _cmp_part_kernel, grid=(rows // tm,), in_specs=[xs, xs, ws, ws], out_specs=[os_, os_],
        out_shape=[jax.ShapeDtypeStruct((rows, n), F32)] * 2,
        compiler_params=_cparams(("arbitrary",)), name="cmp_part",
    )(xk, xv, wk, wv)


def _pages_dma(pt_ref, b, chunk, pools, bufs, sems, slot, wait):
    def body(p, carry):
        page = pt_ref[b, chunk * PAGES_PER_STEP + p]
        for k, (pool, buf) in enumerate(zip(pools, bufs)):
            copy = pltpu.make_async_copy(pool.at[page], buf.at[slot, p], sems.at[slot, k])
            if wait:
                copy.wait()
            else:
                copy.start()
        return carry

    lax.fori_loop(0, PAGES_PER_STEP, body, 0)


def _gather_pages_pipelined(pt_ref, b, c, n_chunks, pools, bufs, sems):
    step = b * n_chunks + c
    slot = step % 2
    more_in_batch = c + 1 < n_chunks
    next_b = jnp.where(more_in_batch, b, b + 1)
    next_c = jnp.where(more_in_batch, c + 1, 0)

    @pl.when(step == 0)
    def _():
        _pages_dma(pt_ref, b, c, pools, bufs, sems, slot, wait=False)

    @pl.when(more_in_batch | (b + 1 < pl.num_programs(0)))
    def _():
        _pages_dma(pt_ref, next_b, next_c, pools, bufs, sems, 1 - slot, wait=False)

    _pages_dma(pt_ref, b, c, pools, bufs, sems, slot, wait=True)
    return slot


def _cmp_part_paged_kernel(pt_ref, poolk_ref, poolv_ref, perm_ref, wk_ref, wv_ref, pk_ref, pv_ref,
                           bufk, bufv, xk, xv, sems):
    b, c = pl.program_id(0), pl.program_id(1)
    slot = _gather_pages_pipelined(pt_ref, b, c, pl.num_programs(1), (poolk_ref, poolv_ref),
                                   (bufk, bufv), sems)
    cpp = PAGE_SIZE // CMP_STRIDE
    perm = perm_ref[...]

    def page_rows(p, carry):
        r0 = pl.multiple_of(p * cpp, cpp)
        for buf, x in ((bufk, xk), (bufv, xv)):
            y = _dot_nt(perm, buf[slot, p])
            for s in range(CMP_STRIDE):
                x[pl.ds(r0, cpp), s * KV_WIDTH:(s + 1) * KV_WIDTH] = y[s * cpp:(s + 1) * cpp]
        return carry

    lax.fori_loop(0, PAGES_PER_STEP, page_rows, 0, unroll=4)
    pk_ref[0] = _dot(xk[...], wk_ref[...])
    pv_ref[0] = _dot(xv[...], wv_ref[...])


def _cmp_part_paged(page_table, poolk, poolv, wk, wv):
    nb, n_pages = page_table.shape
    cpp = PAGE_SIZE // CMP_STRIDE
    assert cpp == SUBLANES
    n = wk.shape[1]
    rows = PAGES_PER_STEP * cpp
    slot = jnp.arange(PAGE_SIZE)
    perm = (slot[None, :] == (slot[:, None] % cpp) * CMP_STRIDE + slot[:, None] // cpp).astype(MXU_DT)
    any_spec = pl.BlockSpec(memory_space=pl.ANY)
    full = lambda a: pl.BlockSpec(a.shape, lambda b, c, pt: (0, 0))
    os_ = pl.BlockSpec((1, rows, n), lambda b, c, pt: (b, c, 0))
    page_buf = pltpu.VMEM((2, PAGES_PER_STEP, KV_WIDTH, PAGE_SIZE), F32)
    chunk_rows = pltpu.VMEM((rows, CMP_STRIDE * KV_WIDTH), F32)
    grid_spec = pltpu.PrefetchScalarGridSpec(
        num_scalar_prefetch=1, grid=(nb, n_pages // PAGES_PER_STEP),
        in_specs=[any_spec, any_spec, full(perm), full(wk), full(wv)], out_specs=[os_, os_],
        scratch_shapes=[page_buf, page_buf, chunk_rows, chunk_rows, pltpu.SemaphoreType.DMA((2, 2))])
    return pl.pallas_call(
        _cmp_part_paged_kernel, grid_spec=grid_spec,
        out_shape=[jax.ShapeDtypeStruct((nb, n_pages * cpp, n), F32)] * 2,
        compiler_params=_cparams(("arbitrary", "arbitrary")), name="cmp_part_paged",
    )(page_table, poolk, poolv, perm, wk, wv)


def _cmp_finish_kernel(pk_ref, pv_ref, pe_ref, w1k_ref, w1v_ref, w2k_ref, w2v_ref, kc_ref, vc_ref):
    pe = pe_ref[...]
    for p_ref, w1_ref, w2_ref, o_ref in ((pk_ref, w1k_ref, w2k_ref, kc_ref),
                                         (pv_ref, w1v_ref, w2v_ref, vc_ref)):
        p = p_ref[0]
        n_ch = p.shape[0]
        p0, p1 = p[:, :KV_WIDTH], p[:, KV_WIDTH:]
        row = lax.broadcasted_iota(jnp.int32, p1.shape, 0)
        p1_next = jnp.where(row == n_ch - 1, 0.0, pltpu.roll(p1, n_ch - 1, 0))
        bias = jnp.sum(pe * w1_ref[...], axis=0, keepdims=True)
        o_ref[0] = _dot(_silu(p0 + p1_next + bias), w2_ref[...]).astype(o_ref.dtype)


def _cmp_finish(pk, pv, pe_col, w1k_t, w1v_t, w2k_bd, w2v_bd):
    nb, n_ch, n = pk.shape
    ps = pl.BlockSpec((1, n_ch, n), lambda b: (b, 0, 0))
    full = lambda a: pl.BlockSpec(a.shape, lambda b: (0,) * a.ndim)
    os_ = pl.BlockSpec((1, n_ch, KV_WIDTH), lambda b: (b, 0, 0))
    return pl.pallas_call(
        _cmp_finish_kernel, grid=(nb,),
        in_specs=[ps, ps, full(pe_col), full(w1k_t), full(w1v_t), full(w2k_bd), full(w2v_bd)],
        out_specs=[os_, os_],
        out_shape=[jax.ShapeDtypeStruct((nb, n_ch, KV_WIDTH), MXU_DT)] * 2,
        compiler_params=_cparams(("arbitrary",)), name="cmp_finish",
    )(pk, pv, pe_col, w1k_t, w1v_t, w2k_bd, w2v_bd)


def _cmp_weights(w1, w2):
    eye = jnp.eye(N_KV, dtype=F32)
    w1r = w1.reshape(CMP_RATIO, CMP_STRIDE, HEAD_DIM, CMP_HIDDEN)
    big = jnp.einsum('rsdh,gk->sgdrkh', w1r, eye).reshape(CMP_STRIDE * KV_WIDTH, CMP_RATIO * KV_WIDTH)
    w1_t = jnp.tile(w1, (1, N_KV))
    w2_bd = jnp.einsum('hd,gk->ghkd', w2, eye).reshape(N_KV * CMP_HIDDEN, KV_WIDTH)
    return big.astype(MXU_DT), w1_t, w2_bd.astype(MXU_DT)


def _lane_half_mask(shape, half):
    lane = lax.broadcasted_iota(jnp.int32, shape, len(shape) - 1)
    return (lane >= HEAD_DIM) if half else (lane < HEAD_DIM)


def _build_q4(q_ref, q4_ref, tq):
    del tq
    for g in range(N_KV):
        hg = g % 2
        parts = []
        for r in range(GROUP_R):
            h = g * GROUP_R + r
            slab = q_ref[0, :, (h // 2) * LANES:(h // 2 + 1) * LANES].astype(F32)
            if h % 2 != hg:
                slab = pltpu.roll(slab, HEAD_DIM, 1)
            parts.append(jnp.where(_lane_half_mask(slab.shape, hg), slab, 0.0))
        q4_ref[g] = jnp.concatenate(parts, axis=0).astype(q4_ref.dtype)


def _row_qpos(rows, cols, tq, q0):
    t = lax.broadcasted_iota(jnp.int32, (rows, cols), 0) & (tq - 1)
    return q0 + t


def _add_head_bias(s, bias, tq):
    rows, width = s.shape
    return (s.reshape(GROUP_R, tq, width) + bias[None]).reshape(rows, width)


def _softmax_attend(q4, k, v, bias, tq):
    t = _add_head_bias(_dot_nt(q4, k), bias, tq)
    m = jnp.max(t, axis=1, keepdims=True)
    e = jnp.exp2(t - m)
    l = jnp.sum(e, axis=1, keepdims=True)
    inv = jnp.where(m > 0.5 * NEG, 1.0 / l, 0.0)
    return _dot(e, v) * inv, e, inv


def _top_blocks(imp, qpos, n_blocks, axis):
    j = lax.broadcasted_iota(jnp.int32, imp.shape, axis)
    cur = qpos >> SEL_SHIFT
    forced = (j == 0) | (j == cur) | (j == cur - 1)
    valid = j * SEL_LEN <= qpos
    v = jnp.where(valid, imp + jnp.where(forced, SEL_FORCE, 0.0), -SEL_FORCE)
    v = jnp.where(j < n_blocks, v, -jnp.inf)
    sel = jnp.zeros(imp.shape, F32)
    jf = j.astype(F32)
    for _ in range(N_SEL):
        m = jnp.max(v, axis=axis, keepdims=True)
        first = jnp.min(jnp.where(v == m, jf, float(imp.shape[axis])), axis=axis, keepdims=True)
        hit = jf == first
        sel = jnp.where(hit, 1.0, sel)
        v = jnp.where(hit, -jnp.inf, v)
    return sel


def _compressed_and_select(q4_ref, kc_ref, vc_ref, ov_ref, oc_ref, selneg_ref, tq, q0, n_blocks,
                           blocks_on_rows, ncp=None, block_rows=None):
    ncp = kc_ref.shape[1] if ncp is None else ncp
    n = lax.broadcasted_iota(jnp.int32, (tq, ncp), 1)
    bias_c = jnp.where(n * CMP_STRIDE + (CMP_LEN - 1) <= _row_qpos(tq, ncp, tq, q0), 0.0, NEG)
    psums = []
    for g in range(N_KV):
        sl = slice((g // 2) * LANES, (g // 2 + 1) * LANES)
        o, e, inv = _softmax_attend(q4_ref[g], kc_ref[0, :ncp, sl], vc_ref[0, :ncp, sl], bias_c, tq)
        oc_ref[g] = o
        p = e * inv
        ps = p[0:tq]
        for r in range(1, GROUP_R):
            ps = ps + p[r * tq:(r + 1) * tq]
        psums.append(ps)
    psum = jnp.concatenate(psums, axis=0)
    if blocks_on_rows:
        nb = ov_ref.shape[0]
        block_rows = nb if block_rows is None else block_rows
        ov_t = ov_ref[:block_rows, :ncp]
        imp_t = sum(_dot_nt(ov_t, piece) for piece in _split3(psum))
        qpos_t = q0 + (lax.broadcasted_iota(jnp.int32, imp_t.shape, 1) & (tq - 1))
        sel_t = _top_blocks(imp_t, qpos_t, min(n_blocks, block_rows), 0)
        if block_rows < nb:
            sel_t = jnp.concatenate([sel_t, jnp.zeros((nb - block_rows, sel_t.shape[1]), F32)], axis=0)
        sel = sel_t.T
    else:
        imp = _dot_exact_rhs(psum, ov_ref[...])
        sel = _top_blocks(imp, _row_qpos(N_KV * tq, imp.shape[1], tq, q0), n_blocks, 1)
    selneg_ref[...] = jnp.where(sel > 0.5, 0.0, NEG).astype(selneg_ref.dtype)


def _build_q4_masked(q4_ref, selneg_ref, q4w_ref, tq):
    n_windows = q4w_ref.shape[0]
    for g in range(N_KV):
        hg = g % 2
        q4 = q4_ref[g].astype(F32)
        mask_rows = selneg_ref[g * tq:(g + 1) * tq, :].astype(F32)
        for w in range(n_windows):
            slab = mask_rows[:, (w // 2) * LANES:(w // 2 + 1) * LANES]
            if w % 2 == hg:
                slab = pltpu.roll(slab, HEAD_DIM, 1)
            slab = jnp.where(_lane_half_mask(slab.shape, 1 - hg), slab, 0.0)
            q4w_ref[w, g] = (q4 + jnp.concatenate([slab] * GROUP_R, axis=0)).astype(q4w_ref.dtype)


def _selected_tile_update(q4w_ref, ks_ref, vs_ref, key0, first_block, tq, q0, m_ref, acc_ref):
    window = first_block >> SEL_SHIFT
    shape = (KEY_TILE, LANES)
    key_row = lax.broadcasted_iota(jnp.int32, shape, 0)
    lane = lax.broadcasted_iota(jnp.int32, shape, 1)
    block_in_window = first_block - window * HEAD_DIM + (key_row >> SEL_SHIFT)
    block_one_hot = jnp.where((lane & (HEAD_DIM - 1)) == block_in_window, 1.0, 0.0).astype(MXU_DT)
    kpos = key0 + lax.broadcasted_iota(jnp.int32, (tq, KEY_TILE), 1)
    causal = jnp.where(kpos <= _row_qpos(tq, KEY_TILE, tq, q0), 0.0, NEG)
    for g in range(N_KV):
        sl = slice((g // 2) * LANES, (g // 2 + 1) * LANES)
        own = _lane_half_mask(shape, g % 2)
        k1 = jnp.where(own, ks_ref[0, pl.ds(key0, KEY_TILE), sl], block_one_hot)
        v1 = _ones_in_other_half(vs_ref[0, pl.ds(key0, KEY_TILE), sl], g % 2, 1)
        t = _add_head_bias(_dot_nt(q4w_ref[window, g], k1), causal, tq)
        m_old = m_ref[g]
        m_new = jnp.maximum(m_old, jnp.max(t, axis=1, keepdims=True))
        alpha = jnp.exp2(m_old - m_new)
        e = jnp.concatenate([jnp.exp2(t[:, c * LANES:(c + 1) * LANES] - m_new)
                             for c in range(KEY_TILE // LANES)], axis=1).astype(MXU_DT)
        acc_ref[g] = alpha * acc_ref[g] + jnp.dot(e, v1, preferred_element_type=F32)
        m_ref[g] = m_new


def _init_softmax_state(m_ref, acc_ref):
    m_ref[...] = jnp.full(m_ref.shape, NEG, F32)
    acc_ref[...] = jnp.zeros(acc_ref.shape, F32)


def _gate_select_matrix():
    c = jnp.arange(LANES)[None, :, None]
    g = jnp.arange(N_KV)[:, None, None]
    blk = jnp.arange(GROUP_R * 3 * LANES)[None, None, :] // LANES
    sel = c == 3 * (GROUP_R * g + blk // 3) + blk % 3
    return jnp.concatenate([sel, sel], axis=1).astype(MXU_DT)


def _combine_and_store(gl_ref, gsel_ref, oc_ref, ow_ref, selected_of, o_ref, tq):
    gates = _sigmoid(gl_ref[0])
    g_hi = gates.astype(MXU_DT)
    g_lo = (gates - g_hi.astype(F32)).astype(MXU_DT)
    gate_pieces = jnp.concatenate([g_hi, g_lo], axis=1)
    for g in range(N_KV):
        hg = g % 2
        o_s = selected_of(g)
        o_c, o_w = oc_ref[g], ow_ref[g]
        lane_gates = jnp.dot(gate_pieces, gsel_ref[g], preferred_element_type=F32)
        heads = []
        for r in range(GROUP_R):
            h = g * GROUP_R + r
            rs = slice(r * tq, (r + 1) * tq)
            gc = [lane_gates[:, (3 * r + j) * LANES:(3 * r + j + 1) * LANES] for j in range(3)]
            x = gc[0] * o_c[rs] + gc[1] * o_s[rs] + gc[2] * o_w[rs]
            if h % 2 != hg:
                x = pltpu.roll(x, HEAD_DIM, 1)
            heads.append(x)
        for i in range(GROUP_R // 2):
            slab = jnp.where(_lane_half_mask(heads[0].shape, 0), heads[2 * i], heads[2 * i + 1])
            lo = (g * GROUP_R // 2 + i) * LANES
            o_ref[0, :, lo:lo + LANES] = slab


def _window_attend(q4_ref, kw, vw, bias, ow_ref, tq):
    for g in range(N_KV):
        sl = slice((g // 2) * LANES, (g // 2 + 1) * LANES)
        ow_ref[g] = _softmax_attend(q4_ref[g], kw[:, sl], vw[:, sl], bias, tq)[0]


def _attn_scratch(rows, tq, nbp):
    per_group = lambda dt: pltpu.VMEM((N_KV, rows, LANES), dt)
    return [per_group(MXU_DT), per_group(F32), per_group(F32), pltpu.VMEM((N_KV * tq, nbp), MXU_DT),
            per_group(F32), per_group(F32)]


def _ones_in_other_half(v, hg, axis):
    idx = lax.broadcasted_iota(jnp.int32, v.shape, axis)
    own = (idx >= HEAD_DIM) if hg else (idx < HEAD_DIM)
    return jnp.where(own, v, jnp.ones_like(v))


def _attn_prompt_kernel(*refs, tq, n_win):
    (q_ref, gl_ref, gsel_ref, kc_ref, vc_ref, ov_ref, ks_ref, vs_ref) = refs[:8]
    kw_refs = refs[8:8 + n_win]
    vw_refs = refs[8 + n_win:8 + 2 * n_win]
    o_ref = refs[8 + 2 * n_win]
    q4_ref, oc_ref, ow_ref, selneg_ref, m_ref, acc_ref, q4w_ref = refs[9 + 2 * n_win:]
    i = pl.program_id(1)
    q0 = i * tq
    n_blocks = ks_ref.shape[1] // SEL_LEN

    _build_q4(q_ref, q4_ref, tq)
    ncp = kc_ref.shape[1]
    n_variants = max(1, ncp // LANES)
    variant = jnp.minimum((q0 + tq - CMP_STRIDE) // (CMP_STRIDE * LANES), n_variants - 1)
    for v in range(n_variants):
        width = ncp if n_variants == 1 else (v + 1) * LANES
        block_rows = None if n_variants == 1 else width * CMP_STRIDE // SEL_LEN

        @pl.when(variant == v)
        def _():
            _compressed_and_select(q4_ref, kc_ref, vc_ref, ov_ref, oc_ref, selneg_ref, tq, q0, n_blocks, True,
                                   ncp=width, block_rows=block_rows)

    kw = jnp.concatenate([r[0] for r in kw_refs], axis=0)
    vw = jnp.concatenate([r[0] for r in vw_refs], axis=0)
    lw = kw.shape[0]
    kpos = (i - (n_win - 1)) * tq + lax.broadcasted_iota(jnp.int32, (tq, lw), 1)
    qpos = _row_qpos(tq, lw, tq, q0)
    in_window = (kpos <= qpos) & (kpos >= qpos - WINDOW) & (kpos >= 0)
    _window_attend(q4_ref, kw, vw, jnp.where(in_window, 0.0, NEG), ow_ref, tq)

    _init_softmax_state(m_ref, acc_ref)
    _build_q4_masked(q4_ref, selneg_ref, q4w_ref, tq)

    def body(kt, carry):
        k0 = pl.multiple_of(kt * KEY_TILE, KEY_TILE)
        _selected_tile_update(q4w_ref, ks_ref, vs_ref, k0, kt * BLOCKS_PER_TILE, tq, q0, m_ref, acc_ref)
        return carry

    lax.fori_loop(0, (q0 + tq + KEY_TILE - 1) // KEY_TILE, body, 0)

    def selected_of(g):
        acc = acc_ref[g]
        return acc / pltpu.roll(acc, HEAD_DIM, 1)

    _combine_and_store(gl_ref, gsel_ref, oc_ref, ow_ref, selected_of, o_ref, tq)


def _attn_prompt(q, glog, gsel, kc, vc, ov, ks, vs, kw, vw):
    b, t, d = q.shape
    tq = PROMPT_Q_TILE
    n_win = WINDOW // tq + 1
    nbp = ov.shape[0]
    blk = lambda w: pl.BlockSpec((1, tq, w), lambda bi, i: (bi, i, 0))
    per_b = lambda a: pl.BlockSpec((1,) + a.shape[1:], lambda bi, i: (bi, 0, 0))
    resident = lambda a: pl.BlockSpec((1,) + a.shape[1:], lambda bi, i: (bi, 0, 0),
                                      pipeline_mode=pl.Buffered(1))
    win = [pl.BlockSpec((1, tq, KV_WIDTH),
                        functools.partial(lambda bi, i, k: (bi, jnp.maximum(i - (n_win - 1) + k, 0), 0), k=k))
           for k in range(n_win)]
    rows = GROUP_R * tq
    return pl.pallas_call(
        functools.partial(_attn_prompt_kernel, tq=tq, n_win=n_win),
        grid=(b, t // tq),
        in_specs=[blk(d), blk(LANES), pl.BlockSpec(gsel.shape, lambda bi, i: (0, 0, 0)),
                  per_b(kc), per_b(vc), pl.BlockSpec(ov.shape, lambda bi, i: (0, 0)),
                  resident(ks), resident(vs)] + win + win,
        out_specs=blk(d),
        out_shape=jax.ShapeDtypeStruct((b, t, d), F32),
        scratch_shapes=_attn_scratch(rows, tq, nbp)
                       + [pltpu.VMEM((nbp // HEAD_DIM, N_KV, rows, LANES), MXU_DT)],
        compiler_params=_cparams(("arbitrary", "arbitrary")), name="attn_prompt",
    )(q, glog, gsel, kc, vc, ov, ks, vs, *([kw] * n_win), *([vw] * n_win))


def _attn_sample_kernel(pt_ref, q_ref, gl_ref, gsel_ref, kc_ref, vc_ref, ov_ref, kw_ref, vw_ref, kn_ref, vn_ref,
                        poolk_ref, poolv_ref, o_ref,
                        bufk, bufv, sems, q4_ref, oc_ref, ow_ref, selneg_ref, qbd_ref, m_ref, l_ref, acc_ref,
                        *, tq, past, n_chunks):
    b, c = pl.program_id(0), pl.program_id(1)
    q0 = past
    n_blocks = past // SEL_LEN + 1
    rows = GROUP_R * tq

    @pl.when(c < n_chunks)
    def _():
        _gather_pages_pipelined(pt_ref, b, c, n_chunks, (poolk_ref, poolv_ref), (bufk, bufv), sems)

    @pl.when(c == 0)
    def _():
        _build_q4(q_ref, q4_ref, tq)
        for g in range(N_KV):
            own, zero = q4_ref[g], jnp.zeros((rows, LANES), q4_ref.dtype)
            qbd_ref[g * rows:(g + 1) * rows, :] = jnp.concatenate([zero, own] if g // 2 else [own, zero], axis=1)
        _compressed_and_select(q4_ref, kc_ref, vc_ref, ov_ref, oc_ref, selneg_ref, tq, q0, n_blocks, False)
        lw = kw_ref.shape[1]
        kpos = past - WINDOW + lax.broadcasted_iota(jnp.int32, (tq, lw), 1)
        qpos = _row_qpos(tq, lw, tq, q0)
        in_window = (kpos <= qpos) & (kpos >= qpos - WINDOW)
        _window_attend(q4_ref, kw_ref[0], vw_ref[0], jnp.where(in_window, 0.0, NEG), ow_ref, tq)
        m_ref[...] = jnp.full(m_ref.shape, NEG, F32)
        l_ref[...] = jnp.zeros(l_ref.shape, F32)
        acc_ref[...] = jnp.zeros(acc_ref.shape, F32)

    def tile_update(k, v, feature_major, first_block, key0):
        kt = SAMPLE_KEY_TILE
        selneg = selneg_ref[...]
        nb = selneg.shape[1]
        jj = lax.broadcasted_iota(jnp.int32, (nb, kt), 0)
        kk = lax.broadcasted_iota(jnp.int32, (nb, kt), 1)
        expand = jnp.where(jj == first_block + (kk >> SEL_SHIFT), 1.0, 0.0).astype(MXU_DT)
        bias_sel = jnp.dot(selneg, expand, preferred_element_type=F32)
        kpos = key0 + lax.broadcasted_iota(jnp.int32, (tq, kt), 1)
        causal = jnp.where(kpos <= _row_qpos(tq, kt, tq, q0), 0.0, NEG)
        bias = bias_sel.reshape(N_KV, 1, tq, kt) + causal[None, None]
        qbd = qbd_ref[...]
        s = jnp.dot(qbd, k, preferred_element_type=F32) if feature_major else _dot_nt(qbd, k)
        t = (s.reshape(N_KV, GROUP_R, tq, kt) + bias).reshape(N_KV * rows, kt)
        m_old = m_ref[...]
        m_new = jnp.maximum(m_old, jnp.max(t, axis=1, keepdims=True))
        alpha = jnp.exp2(m_old - m_new)
        e = jnp.concatenate([jnp.exp2(t[:, i * LANES:(i + 1) * LANES] - m_new) for i in range(kt // LANES)],
                            axis=1)
        l_ref[...] = alpha * l_ref[...] + jnp.sum(e, axis=1, keepdims=True)
        e = e.astype(MXU_DT)
        pv = _dot_nt(e, v) if feature_major else jnp.dot(e, v, preferred_element_type=F32)
        acc_ref[...] = jnp.concatenate([alpha] * (KV_WIDTH // LANES), axis=1) * acc_ref[...] + pv
        m_ref[...] = m_new

    @pl.when(c < n_chunks)
    def _():
        pages_per_tile = SAMPLE_KEY_TILE // PAGE_SIZE
        slot = (b * n_chunks + c) % 2

        def body(kt, carry):
            p0 = kt * pages_per_tile
            k_t = jnp.concatenate([bufk[slot, p0 + j] for j in range(pages_per_tile)], axis=1).astype(MXU_DT)
            v_t = jnp.concatenate([bufv[slot, p0 + j] for j in range(pages_per_tile)], axis=1).astype(MXU_DT)
            page0 = c * PAGES_PER_STEP + p0
            tile_update(k_t, v_t, True, page0 * (PAGE_SIZE // SEL_LEN), page0 * PAGE_SIZE)
            return carry

        lax.fori_loop(0, PAGES_PER_STEP // pages_per_tile, body, 0)

    @pl.when(c == n_chunks)
    def _():
        tile_update(kn_ref[0], vn_ref[0], False, past // SEL_LEN, past)
        inv = 1.0 / l_ref[...]

        def selected_of(g):
            sl = slice((g // 2) * LANES, (g // 2 + 1) * LANES)
            return acc_ref[g * rows:(g + 1) * rows, sl] * inv[g * rows:(g + 1) * rows]

        _combine_and_store(gl_ref, gsel_ref, oc_ref, ow_ref, selected_of, o_ref, tq)


def _attn_sample(page_table, q, glog, gsel, kc, vc, ov, kw_full, vw_full, k_new, v_new, poolk, poolv):
    b, tq, d = q.shape
    n_pages = page_table.shape[1]
    past = n_pages * PAGE_SIZE
    n_chunks = n_pages // PAGES_PER_STEP
    rows = GROUP_R * tq
    nbp = ov.shape[1]
    per_b = lambda a: pl.BlockSpec((1,) + a.shape[1:], lambda bi, c, pt: (bi, 0, 0))
    any_spec = pl.BlockSpec(memory_space=pl.ANY)
    grid_spec = pltpu.PrefetchScalarGridSpec(
        num_scalar_prefetch=1, grid=(b, n_chunks + 1),
        in_specs=[per_b(q), per_b(glog), pl.BlockSpec(gsel.shape, lambda bi, c, pt: (0, 0, 0)),
                  per_b(kc), per_b(vc), pl.BlockSpec(ov.shape, lambda bi, c, pt: (0, 0)),
                  per_b(kw_full), per_b(vw_full), per_b(k_new), per_b(v_new), any_spec, any_spec],
        out_specs=per_b(q),
        scratch_shapes=[pltpu.VMEM((2, PAGES_PER_STEP, KV_WIDTH, PAGE_SIZE), F32),
                        pltpu.VMEM((2, PAGES_PER_STEP, KV_WIDTH, PAGE_SIZE), F32),
                        pltpu.SemaphoreType.DMA((2, 2))] + _attn_scratch(rows, tq, nbp)[:4]
                       + [pltpu.VMEM((N_KV * rows, KV_WIDTH), MXU_DT), pltpu.VMEM((N_KV * rows, LANES), F32),
                          pltpu.VMEM((N_KV * rows, LANES), F32), pltpu.VMEM((N_KV * rows, KV_WIDTH), F32)])
    return pl.pallas_call(
        functools.partial(_attn_sample_kernel, tq=tq, past=past, n_chunks=n_chunks),
        grid_spec=grid_spec,
        out_shape=jax.ShapeDtypeStruct((b, tq, d), F32),
        compiler_params=_cparams(("arbitrary", "arbitrary")), name="attn_sample",
    )(page_table, q, glog, gsel, kc, vc, ov, kw_full, vw_full, k_new, v_new, poolk, poolv)


def _win_update_kernel(ck_ref, cv_ref, nk_ref, nv_ref, ok_ref, ov_ref, fk_ref, fv_ref):
    t = nk_ref.shape[1]
    wb = ck_ref.shape[1]
    for c_ref, n_ref, o_ref, f_ref in ((ck_ref, nk_ref, ok_ref, fk_ref), (cv_ref, nv_ref, ov_ref, fv_ref)):
        o_ref[0, 0:wb - t, :] = c_ref[0, t:wb, :]
        o_ref[0, wb - t:wb, :] = n_ref[0]
        pad = jnp.zeros((f_ref.shape[1] - wb - t, f_ref.shape[2]), F32)
        f_ref[0] = jnp.concatenate([c_ref[0], n_ref[0], pad], axis=0).astype(f_ref.dtype)


def _win_update(cache_k, cache_v, new_k, new_v):
    b, wb, w = cache_k.shape
    t = new_k.shape[1]
    full_len = wb + LANES
    cs = pl.BlockSpec((1, wb, w), lambda i: (i, 0, 0))
    ns = pl.BlockSpec((1, t, w), lambda i: (i, 0, 0))
    fs = pl.BlockSpec((1, full_len, w), lambda i: (i, 0, 0))
    return pl.pallas_call(
        _win_update_kernel, grid=(b,), in_specs=[cs, cs, ns, ns], out_specs=[cs, cs, fs, fs],
        out_shape=[jax.ShapeDtypeStruct((b, wb, w), F32)] * 2
                  + [jax.ShapeDtypeStruct((b, full_len, w), MXU_DT)] * 2,
        compiler_params=_cparams(("arbitrary",)), name="win_update",
    )(cache_k, cache_v, new_k, new_v)


def _overlap_matrix(n_cmp_pad, n_cmp, n_blocks_pad, n_blocks):
    n = jnp.arange(n_cmp_pad)[:, None]
    j = jnp.arange(n_blocks_pad)[None, :]
    c_start, s_start = n * CMP_STRIDE, j * SEL_LEN
    ov = (c_start < s_start + SEL_LEN) & (c_start + CMP_LEN > s_start) & (n < n_cmp) & (j < n_blocks)
    return ov.astype(MXU_DT)


def _finish_rows(a, w_ref, gpost_ref, x_ref, gate_ref, y_ref):
    o = jnp.dot(a.astype(MXU_DT), w_ref[...], preferred_element_type=F32)
    n = o * lax.rsqrt(jnp.mean(o * o, axis=-1, keepdims=True) + RMS_EPS) * gpost_ref[...]
    y_ref[0] = x_ref[0] + gate_ref[0] * n


def _attn_out_kernel(o_ref, z_ref, w_ref, gpost_ref, x_ref, gate_ref, y_ref):
    _finish_rows(o_ref[0] * _silu(z_ref[0]), w_ref, gpost_ref, x_ref, gate_ref, y_ref)


def _ssm_out_kernel(y_ref, z_ref, wglu_ref, w_ref, gpost_ref, x_ref, gate_ref, o_ref):
    v = jnp.dot(y_ref[0].astype(MXU_DT), wglu_ref[...], preferred_element_type=F32)
    half = v.shape[1] // 2
    glu = v[:, :half] * _sigmoid(v[:, half:])
    _finish_rows(glu * _silu(z_ref[0]), w_ref, gpost_ref, x_ref, gate_ref, o_ref)


def _mixer_out(body, name, acts, weights, g_post, x, gate):
    b, t, d = x.shape
    tm = min(t, 256)
    row = pl.BlockSpec((1, tm, d), lambda i, j: (i, j, 0))
    full = lambda a: pl.BlockSpec(a.shape, lambda i, j: (0,) * a.ndim)
    per_b = pl.BlockSpec((1, 1, d), lambda i, j: (i, 0, 0))
    return pl.pallas_call(
        body, grid=(b, t // tm),
        in_specs=[row] * len(acts) + [full(w) for w in weights] + [full(g_post), row, per_b],
        out_specs=row, out_shape=jax.ShapeDtypeStruct((b, t, d), F32),
        compiler_params=_cparams(("arbitrary", "arbitrary")), name=name,
    )(*acts, *weights, g_post, x, gate)


def _ssm_proj_kernel(x_ref, shift_ref, scale_ref, g_ref, w_ref, u_ref, z_ref):
    h = _rms_mod(x_ref[0], g_ref[...], scale_ref[0], shift_ref[0]).astype(MXU_DT)
    p = jnp.dot(h, w_ref[...], preferred_element_type=F32)
    half = p.shape[1] // 2
    u_ref[0] = p[:, :half]
    z_ref[0] = p[:, half:]


def _ssm_proj(x, shift, scale, g_pre, w):
    b, t, d = x.shape
    tm = min(t, 256)
    row = pl.BlockSpec((1, tm, d), lambda i, j: (i, j, 0))
    per_b = pl.BlockSpec((1, 1, d), lambda i, j: (i, 0, 0))
    full = lambda a: pl.BlockSpec(a.shape, lambda i, j: (0,) * a.ndim)
    out = jax.ShapeDtypeStruct((b, t, d), F32)
    return pl.pallas_call(
        _ssm_proj_kernel, grid=(b, t // tm),
        in_specs=[row, per_b, per_b, full(g_pre), full(w)], out_specs=[row, row], out_shape=[out, out],
        compiler_params=_cparams(("arbitrary", "arbitrary")), name="ssm_proj",
    )(x, shift, scale, g_pre, w)


def _s5_prep_kernel(ar_ref, ai_ref, ldt_ref, bre_ref, bim_ref, lre_ref, lim_ref, cbre_ref, cbim_ref):
    ar, ai = ar_ref[...], ai_ref[...]
    dt = jnp.exp(ldt_ref[...])
    mag = jnp.exp(dt * ar)
    lam_re, lam_im = mag * jnp.cos(dt * ai), mag * jnp.sin(dt * ai)
    den = ar * ar + ai * ai
    nr, ni = lam_re - 1.0, lam_im
    coef_re, coef_im = (nr * ar + ni * ai) / den, (ni * ar - nr * ai) / den
    lre_ref[...] = lam_re
    lim_ref[...] = lam_im
    cr, ci = coef_re[:, None, :], coef_im[:, None, :]
    b_re, b_im = bre_ref[...], bim_ref[...]
    cbre_ref[...] = cr * b_re - ci * b_im
    cbim_ref[...] = cr * b_im + ci * b_re


def _s5_prep(a_re, a_im, log_dt, b_re_t, b_im_t):
    ng, p = a_re.shape
    full = lambda a: pl.BlockSpec(a.shape, lambda: (0,) * a.ndim)
    args = (a_re, a_im, log_dt.reshape(ng, 1), b_re_t, b_im_t)
    lam = jax.ShapeDtypeStruct((ng, p), F32)
    cb = jax.ShapeDtypeStruct(b_re_t.shape, F32)
    return pl.pallas_call(
        _s5_prep_kernel, in_specs=[full(a) for a in args],
        out_specs=[full(lam), full(lam), full(cb), full(cb)], out_shape=[lam, lam, cb, cb],
        name="s5_prep",
    )(*args)


SCAN_LANES = 512


def _s5_scan_kernel(u_ref, d_ref, bre_ref, bim_ref, cre_ref, cim_ref, lre_ref, lim_ref, h0re_ref, h0im_ref,
                    y_ref, hre_ref, him_ref, sre, sim, *, n_batch, nbp, tt):
    ti = pl.program_id(1)
    n_lc = sre.shape[0]

    @pl.when(ti == 0)
    def _():
        hre_ref[...] = h0re_ref[...]
        him_ref[...] = h0im_ref[...]
        if nbp != n_batch:
            sre[...] = jnp.zeros(sre.shape, F32)
            sim[...] = jnp.zeros(sim.shape, F32)

    def drive(b, carry):
        ub = u_ref[b].astype(MXU_DT)
        for s_ref, w_ref in ((sre, bre_ref), (sim, bim_ref)):
            bb = jnp.dot(ub, w_ref[0], preferred_element_type=F32)
            for c in range(n_lc):
                s_ref[c, pl.ds(b, tt, stride=nbp), :] = bb[:, c * LANES:(c + 1) * LANES]
        return carry

    lax.fori_loop(0, n_batch, drive, 0)

    per_scan = SCAN_LANES // LANES
    for sb in range(nbp // SUBLANES):
        for lc in range(n_lc // per_scan):
            chunks = range(lc * per_scan, (lc + 1) * per_scan)
            srow = slice(sb * SUBLANES, (sb + 1) * SUBLANES)
            lane = lambda c: slice(c * LANES, (c + 1) * LANES)
            lam_re = [jnp.broadcast_to(lre_ref[:, lane(c)], (SUBLANES, LANES)) for c in chunks]
            lam_im = [jnp.broadcast_to(lim_ref[:, lane(c)], (SUBLANES, LANES)) for c in chunks]

            def step(t, carry):
                r0 = pl.multiple_of(t * nbp + sb * SUBLANES, SUBLANES)
                new = []
                for k, c in enumerate(chunks):
                    h_re, h_im = carry[k]
                    n_re = lam_re[k] * h_re - lam_im[k] * h_im + sre[c, pl.ds(r0, SUBLANES), :]
                    n_im = lam_re[k] * h_im + lam_im[k] * h_re + sim[c, pl.ds(r0, SUBLANES), :]
                    sre[c, pl.ds(r0, SUBLANES), :] = n_re
                    sim[c, pl.ds(r0, SUBLANES), :] = n_im
                    new.append((n_re, n_im))
                return tuple(new)

            init = tuple((hre_ref[srow, lane(c)], him_ref[srow, lane(c)]) for c in chunks)
            final = lax.fori_loop(0, tt, step, init, unroll=min(tt, 8))
            for k, c in enumerate(chunks):
                hre_ref[srow, lane(c)] = final[k][0]
                him_ref[srow, lane(c)] = final[k][1]

    def readout(b, carry):
        rows = lambda s_ref: jnp.concatenate(
            [s_ref[c, pl.ds(b, tt, stride=nbp), :] for c in range(n_lc)], axis=1).astype(MXU_DT)
        y = (jnp.dot(rows(sre), cre_ref[0], preferred_element_type=F32)
             - jnp.dot(rows(sim), cim_ref[0], preferred_element_type=F32))
        y_ref[b] = y + d_ref[...] * u_ref[b]
        return carry

    lax.fori_loop(0, n_batch, readout, 0)


def _s5_scan(u, d_skip, b_re_bd, b_im_bd, c_re_bd, c_im_bd, lam_re, lam_im, h0_re, h0_im, nbp):
    b, t, w = u.shape
    q, wq, sq = b_re_bd.shape
    ns = q * sq
    tt = min(t, 256)
    ublk = pl.BlockSpec((b, tt, wq), lambda k, i: (0, i, k))
    dblk = pl.BlockSpec((1, wq), lambda k, i: (0, k))
    bblk = pl.BlockSpec((1, wq, sq), lambda k, i: (k, 0, 0))
    cblk = pl.BlockSpec((1, sq, wq), lambda k, i: (k, 0, 0))
    lblk = pl.BlockSpec((1, sq), lambda k, i: (0, k))
    hblk = pl.BlockSpec((nbp, sq), lambda k, i: (0, k))
    st = jax.ShapeDtypeStruct((nbp, ns), F32)
    return pl.pallas_call(
        functools.partial(_s5_scan_kernel, n_batch=b, nbp=nbp, tt=tt),
        grid=(q, t // tt),
        in_specs=[ublk, dblk, bblk, bblk, cblk, cblk, lblk, lblk, hblk, hblk],
        out_specs=[ublk, hblk, hblk],
        out_shape=[jax.ShapeDtypeStruct((b, t, w), F32), st, st],
        scratch_shapes=[pltpu.VMEM((sq // LANES, tt * nbp, LANES), F32)] * 2,
        compiler_params=_cparams(("arbitrary", "arbitrary")), name="s5_scan",
    )(u, d_skip, b_re_bd, b_im_bd, c_re_bd, c_im_bd, lam_re, lam_im, h0_re, h0_im)


def _s5_block_diag(cb_re, cb_im, c_re, c_im):
    ng, c, p = cb_re.shape
    gq = ng // S5_QUARTERS
    eye = jnp.eye(gq, dtype=F32)

    def drive(w):
        w4 = w.reshape(S5_QUARTERS, gq, c, p)
        return jnp.einsum('qgcp,gk->qgckp', w4, eye).reshape(S5_QUARTERS, gq * c, gq * p).astype(MXU_DT)

    def readout(w):
        w4 = w.reshape(S5_QUARTERS, gq, c, p)
        return jnp.einsum('qgcp,gk->qgpkc', w4, eye).reshape(S5_QUARTERS, gq * p, gq * c).astype(MXU_DT)

    return drive(cb_re), drive(cb_im), readout(c_re), readout(c_im)


def _nsa_layer(xp, xs, modp, mods, caches, page_table, g_pre, g_post, w_in, pe, wk1, wk2, wv1, wv2, w_out):
    cache_kc, cache_vc, cache_ks, cache_vs, cache_kw, cache_vw = caches
    b, t, d = xp.shape
    sb, st, _ = xs.shape
    n_pages = page_table.shape[1]
    past = n_pages * PAGE_SIZE
    assert st < CMP_STRIDE and n_pages % PAGES_PER_STEP == 0 and t % KEY_TILE == 0

    aw = N_HEADS * HEAD_DIM
    cuts = [aw, aw + 6 * KV_WIDTH, aw + 6 * KV_WIDTH + 3 * N_HEADS]
    wq = w_in[:, :cuts[0]].astype(MXU_DT)
    wkv = w_in[:, cuts[0]:cuts[1]].astype(MXU_DT)
    wg = jnp.pad(w_in[:, cuts[1]:cuts[2]], ((0, 0), (0, LANES - 3 * N_HEADS))).astype(MXU_DT)
    wz = w_in[:, cuts[2]:].astype(MXU_DT)
    g_pre2, g_post2 = g_pre.reshape(1, d), g_post.reshape(1, d)

    proj_p = _attn_proj(xp, modp[0], modp[1], g_pre2, _rope_tables(jnp.arange(t)), wq, wkv, wg, wz)
    proj_s = _attn_proj(xs, mods[0], mods[1], g_pre2, _rope_tables(past + jnp.arange(st)), wq, wkv, wg, wz)
    (q_p, kc_p, vc_p, ks_p, vs_p, kw_p, vw_p, ksb_p, vsb_p, kwb_p, vwb_p, gl_p, z_p) = proj_p
    (q_s, kc_s, vc_s, ks_s, vs_s, kw_s, vw_s, ksb_s, vsb_s, _, _, gl_s, z_s) = proj_s

    wk_big, w1k_t, w2k_bd = _cmp_weights(wk1, wk2)
    wv_big, w1v_t, w2v_bd = _cmp_weights(wv1, wv2)
    pe_col = pe.reshape(CMP_LEN * HEAD_DIM, 1)
    chunk_w = CMP_STRIDE * KV_WIDTH

    n_ch = t // CMP_STRIDE
    pk, pv = _cmp_part(kc_p.reshape(b * n_ch, chunk_w), vc_p.reshape(b * n_ch, chunk_w), wk_big, wv_big)
    kcc_p, vcc_p = _cmp_finish(pk.reshape(b, n_ch, -1), pv.reshape(b, n_ch, -1), pe_col,
                               w1k_t, w1v_t, w2k_bd, w2v_bd)
    ns_p = -(-t // SEL_LEN)
    nbp_p = -(-ns_p // LANES) * LANES
    ov_p = _overlap_matrix(n_ch, n_ch - CMP_RATIO + 1, nbp_p, ns_p).T
    gsel = _gate_select_matrix()
    o_p = _attn_prompt(q_p, gl_p, gsel, kcc_p, vcc_p, ov_p, ksb_p, vsb_p, kwb_p, vwb_p)
    y_p = _mixer_out(_attn_out_kernel, "attn_out", (o_p, z_p), (w_out.astype(MXU_DT),), g_post2, xp, modp[2])

    cpp = PAGE_SIZE // CMP_STRIDE
    n_phys = cache_kc.shape[0]
    feature_major = lambda pool: pool.transpose(0, 2, 3, 1).reshape(n_phys, KV_WIDTH, PAGE_SIZE)
    pk, pv = _cmp_part_paged(page_table, feature_major(cache_kc), feature_major(cache_vc), wk_big, wv_big)
    kcc_s, vcc_s = _cmp_finish(pk, pv, pe_col, w1k_t, w1v_t, w2k_bd, w2v_bd)
    n_ch_s = (past + st) // CMP_STRIDE
    ns_s = -(-(past + st) // SEL_LEN)
    nbp_s = -(-ns_s // LANES) * LANES
    ov_s = _overlap_matrix(n_pages * cpp, n_ch_s - CMP_RATIO + 1, nbp_s, ns_s)
    wb = cache_kw.shape[1]
    win_k, win_v, kw_full, vw_full = _win_update(cache_kw.reshape(sb, wb, KV_WIDTH),
                                                 cache_vw.reshape(sb, wb, KV_WIDTH), kw_s, vw_s)
    pad_new = lambda a: jnp.pad(a, ((0, 0), (0, SAMPLE_KEY_TILE - st), (0, 0)))
    o_s = _attn_sample(page_table, q_s, gl_s, gsel, kcc_s, vcc_s, ov_s, kw_full, vw_full,
                       pad_new(ksb_s), pad_new(vsb_s), feature_major(cache_ks), feature_major(cache_vs))
    y_s = _mixer_out(_attn_out_kernel, "attn_out_s", (o_s, z_s), (w_out.astype(MXU_DT),), g_post2, xs, mods[2])

    heads = lambda a: a.reshape(a.shape[0], a.shape[1], N_KV, HEAD_DIM)
    wl = min(WINDOW, t)
    rows_p = (kc_p, vc_p, ks_p, vs_p, kw_p[:, t - wl:], vw_p[:, t - wl:])
    rows_s = (kc_s, vc_s, ks_s, vs_s, win_k, win_v)
    return y_p, y_s, [heads(a) for a in rows_p], [heads(a) for a in rows_s]


def _s5_layer(xp, xs, modp, mods, state_re, state_im, g_pre, g_post, w_in, a_re, a_im, log_dt,
              b_re, b_im, c_re, c_im, d_skip, w_glu, w_out):
    b, t, d = xp.shape
    sb, st, _ = xs.shape
    ng, p, c = b_re.shape
    g_pre2, g_post2 = g_pre.reshape(1, d), g_post.reshape(1, d)
    w_in16, w_glu16, w_out16 = w_in.astype(MXU_DT), w_glu.astype(MXU_DT), w_out.astype(MXU_DT)
    lam_re, lam_im, cb_re, cb_im = _s5_prep(a_re, a_im, log_dt, b_re.transpose(0, 2, 1), b_im.transpose(0, 2, 1))
    bd = _s5_block_diag(cb_re, cb_im, c_re, c_im)
    lam = (lam_re.reshape(1, ng * p), lam_im.reshape(1, ng * p))
    d2 = d_skip.reshape(1, d)

    outs = []
    for x, mod, h0 in ((xp, modp, None), (xs, mods, (state_re, state_im))):
        nb = x.shape[0]
        nbp = -(-nb // SUBLANES) * SUBLANES
        u, z = _ssm_proj(x, mod[0], mod[1], g_pre2, w_in16)
        if h0 is None:
            h0_re = h0_im = jnp.zeros((nbp, ng * p), F32)
        else:
            h0_re, h0_im = (jnp.pad(h.reshape(nb, ng * p), ((0, nbp - nb), (0, 0))) for h in h0)
        y, h_re, h_im = _s5_scan(u, d2, *bd, *lam, h0_re, h0_im, nbp)
        out = _mixer_out(_ssm_out_kernel, "ssm_out", (y, z), (w_glu16, w_out16), g_post2, x, mod[2])
        outs.append((out, h_re[:nb].reshape(nb, ng, p), h_im[:nb].reshape(nb, ng, p)))
    return outs


def kernel(x_prompt, x_sample, cache_k_cmp, cache_v_cmp, cache_k_sel, cache_v_sel, cache_k_win, cache_v_win, state_s5_re, state_s5_im, page_table, c_prompt, c_sample, norm_pre, norm_post, w_ada, b_ada, w_in_attn, pe_cmp, w_cmp_k1, w_cmp_k2, w_cmp_v1, w_cmp_v2, w_out_attn, w_in_ssm, s5_a_re, s5_a_im, s5_log_dt, s5_b_re, s5_b_im, s5_c_re, s5_c_im, s5_d, w_glu, w_out_ssm):
    b, t, d = x_prompt.shape
    sb = x_sample.shape[0]
    depth = w_ada.shape[0]
    c_all = jnp.concatenate([c_prompt, c_sample], axis=0)
    n_c = -(-c_all.shape[0] // SUBLANES) * SUBLANES
    mod = _ada(jnp.pad(c_all, ((0, n_c - c_all.shape[0]), (0, 0))), w_ada, b_ada)

    yp, ys = x_prompt, x_sample
    attn_p, attn_s, ssm_p, ssm_s = [], [], [], []
    for i in range(depth):
        parts = [mod[i, :, k * d:(k + 1) * d] for k in range(3)]
        modp = [a[:b].reshape(b, 1, d) for a in parts]
        mods = [a[b:b + sb].reshape(sb, 1, d) for a in parts]
        li = i // 2
        if i % 2 == 0:
            caches = (cache_k_cmp[li], cache_v_cmp[li], cache_k_sel[li], cache_v_sel[li],
                      cache_k_win[li], cache_v_win[li])
            yp, ys, rows_p, rows_s = _nsa_layer(
                yp, ys, modp, mods, caches, page_table, norm_pre[i], norm_post[i], w_in_attn[li], pe_cmp[li],
                w_cmp_k1[li], w_cmp_k2[li], w_cmp_v1[li], w_cmp_v2[li], w_out_attn[li])
            attn_p.append(rows_p)
            attn_s.append(rows_s)
        else:
            (yp, pr, pi), (ys, sr, si) = _s5_layer(
                yp, ys, modp, mods, state_s5_re[li], state_s5_im[li], norm_pre[i], norm_post[i], w_in_ssm[li],
                s5_a_re[li], s5_a_im[li], s5_log_dt[li], s5_b_re[li], s5_b_im[li], s5_c_re[li], s5_c_im[li],
                s5_d[li], w_glu[li], w_out_ssm[li])
            ssm_p.append((pr, pi))
            ssm_s.append((sr, si))

    outs = [yp, ys]
    for n in range(6):
        outs.append(jnp.stack([rows[n] for rows in attn_p]))
        outs.append(jnp.stack([rows[n] for rows in attn_s]))
    for n in range(2):
        outs.append(jnp.stack([st[n] for st in ssm_p]))
        outs.append(jnp.stack([st[n] for st in ssm_s]))
    return tuple(outs)
```
